```python
import math
import jax
import jax.numpy as jnp
from jax import lax
import numpy as np

D_MODEL = 1024
BATCH = 8
SEQ = 4096
DEPTH = 4

GRID_W = 64
CTX_LEN = 256
NORM_EPS = 1e-6
N_BRANCH = 3
D_BRANCH = D_MODEL

D_RNN = D_BRANCH
LRU_BLOCKS = 8
LRU_BLOCK_W = D_RNN // LRU_BLOCKS
LRU_C = 8.0
RNN_CONV = 4

D_HYENA = D_BRANCH
HYENA_ORDER = 2
HYENA_SHORT = 3
HYENA_BANDS = 16
HYENA_EMB = 2 * HYENA_BANDS + 1
HYENA_FFN = 64
HYENA_MIN_DECAY = 3.07
HYENA_MAX_DECAY = 15.35

D_SSM = D_BRANCH
SSM_HEAD_DIM = 64
SSM_HEADS = D_SSM // SSM_HEAD_DIM
SSM_GROUPS = 2
SSM_STATE = 128
SSM_CONV = 4
SSM_CHUNK = 128
D_XBC = D_SSM + 2 * SSM_GROUPS * SSM_STATE

D_FF = -(-8 * D_MODEL // (3 * 256)) * 256

IN_SIZES = (D_RNN, D_RNN, (HYENA_ORDER + 1) * D_HYENA, D_SSM, D_XBC + 2 * SSM_HEADS, N_BRANCH * D_MODEL)
D_IN = sum(IN_SIZES)

kernel_name = 'hybrid_rglru_hyena_ssd_prefix_block'


def rmsnorm(x, g):
    xf = x.astype(jnp.float32)
    y = xf * lax.rsqrt(jnp.mean(xf * xf, axis=-1, keepdims=True) + NORM_EPS)
    return (y * g.astype(jnp.float32)).astype(x.dtype)


def split_in(p):
    offs = np.cumsum(IN_SIZES)[:-1].tolist()
    return jnp.split(p, offs, axis=-1)


def short_conv(u, w, b):
    k, ch = w.shape
    left = k // 2
    y = lax.conv_general_dilated(u, w[:, None, :].astype(u.dtype), window_strides=(1,),
                                 padding=[(left, k - 1 - left)],
                                 dimension_numbers=('NWC', 'WIO', 'NWC'),
                                 feature_group_count=ch)
    return y + b.astype(u.dtype)


def flip1(t):
    return jnp.flip(t, axis=1)


def to_col_major(u):
    bsz, n, ch = u.shape
    rows = n // GRID_W
    return u.reshape(bsz, rows, GRID_W, ch).transpose(0, 2, 1, 3).reshape(bsz, n, ch)


def to_row_major(u):
    bsz, n, ch = u.shape
    rows = n // GRID_W
    return u.reshape(bsz, GRID_W, rows, ch).transpose(0, 2, 1, 3).reshape(bsz, n, ch)


def _lin_combine(l, r):
    return (l[0] * r[0], r[0] * l[1] + r[1])


def linear_scan(a, b, h0, reverse):
    a_c, b_c = lax.associative_scan(_lin_combine, (a, b), reverse=reverse, axis=1)
    return b_c + a_c * h0[:, None]


def rglru_coeffs(u, w_a, b_a, w_x, b_x, lam):
    bsz, n, _ = u.shape
    uf = u.astype(jnp.float32)
    ub = uf.reshape(bsz, n, LRU_BLOCKS, LRU_BLOCK_W)
    r = jax.nn.sigmoid(jnp.einsum('blhi,dhij->dblhj', ub, w_a.astype(jnp.float32)).reshape(2, bsz, n, D_RNN)
                       + b_a.astype(jnp.float32)[:, None, None])
    i = jax.nn.sigmoid(jnp.einsum('blhi,dhij->dblhj', ub, w_x.astype(jnp.float32)).reshape(2, bsz, n, D_RNN)
                       + b_x.astype(jnp.float32)[:, None, None])
    log_a = -LRU_C * r * jax.nn.softplus(-lam.astype(jnp.float32))[:, None, None]
    a = jnp.exp(log_a)
    gated = jnp.sqrt(-jnp.expm1(2.0 * log_a)) * (i * uf[None])
    return a, gated


def rglru_mixer(x_c, x_l, gate_c, gate_l, conv_w, conv_b, w_a, b_a, w_x, b_x, lam, ctx_out):
    bsz = x_l.shape[0]
    h0 = jnp.zeros((bsz, D_RNN), jnp.float32)
    a_c, v_c = rglru_coeffs(short_conv(x_c, conv_w, conv_b), w_a, b_a, w_x, b_x, lam)
    hf_c = linear_scan(a_c[0], v_c[0], h0, False)
    hb_c = linear_scan(a_c[1], v_c[1], h0, True)
    a_l, v_l = rglru_coeffs(short_conv(x_l, conv_w, conv_b), w_a, b_a, w_x, b_x, lam)
    hf_l = linear_scan(a_l[0], v_l[0], hf_c[:, -1], False)
    hb_l = linear_scan(a_l[1], v_l[1], hb_c[:, 0], True)
    y_l = (hf_l + hb_l).astype(x_l.dtype) * jax.nn.gelu(gate_l)
    y_c = (hf_c + hb_c).astype(x_c.dtype) * jax.nn.gelu(gate_c) if ctx_out else None
    return y_c, y_l


def hyena_filter_spectrum(n, w1, b1, w2, b2, w3, freq, decay):
    f32 = jnp.float32
    t = jnp.linspace(0.0, 1.0, n, dtype=f32)[:, None]
    bands = jnp.linspace(1e-4, HYENA_BANDS - 1, HYENA_BANDS, dtype=f32)
    ang = (2.0 * math.pi / n) * jnp.arange(n, dtype=f32)[:, None] * bands[None]
    emb = jnp.concatenate([t, jnp.cos(ang), jnp.sin(ang)], axis=-1)
    fr = freq.astype(f32)
    h = jnp.sin(fr * (emb @ w1.astype(f32) + b1.astype(f32)))
    h = jnp.sin(fr * (h @ w2.astype(f32) + b2.astype(f32)))
    h = (h @ w3.astype(f32)) * jnp.exp(-t * decay.astype(f32))
    h = h.reshape(n, 2, HYENA_ORDER, D_HYENA)
    k = jnp.concatenate([h[:, 0], jnp.zeros((1, HYENA_ORDER, D_HYENA), f32), h[:0:-1, 1]], axis=0)
    k = k / jnp.sum(jnp.abs(k), axis=0, keepdims=True)
    return jnp.fft.rfft(k, axis=0)


def hyena_mixer(p, short_w, short_b, w1, b1, w2, b2, w3, freq, decay, bias):
    n = p.shape[1]
    u = short_conv(p, short_w, short_b).astype(jnp.float32)
    parts = jnp.split(u, HYENA_ORDER + 1, axis=-1)
    k_f = hyena_filter_spectrum(n, w1, b1, w2, b2, w3, freq, decay)
    bias = bias.astype(jnp.float32)
    z = parts[0]
    for o in range(HYENA_ORDER):
        z_f = jnp.fft.rfft(z, n=2 * n, axis=1)
        conv = jnp.fft.irfft(z_f * k_f[:, o], n=2 * n, axis=1)[:, :n]
        z = parts[o + 1] * (conv + bias[o] * z)
    return z.astype(p.dtype)


def ssd_scan(xs, dt, a, bm, cm, h0, with_y):
    bsz, n = xs.shape[0], xs.shape[1]
    nc = n // SSM_CHUNK
    e = SSM_HEADS // SSM_GROUPS
    shp = (bsz, nc, SSM_CHUNK, SSM_GROUPS)
    x = xs.reshape(shp + (e, SSM_HEAD_DIM))
    dtc = dt.reshape(shp + (e,))
    bc = bm.reshape(shp + (SSM_STATE,))
    cc = cm.reshape(shp + (SSM_STATE,))
    a_cum = jnp.cumsum(dtc * a.reshape(SSM_GROUPS, e), axis=2)
    xdt = x * dtc[..., None]
    states = jnp.einsum('bclgn,bclge,bclgep->bcgepn', bc, jnp.exp(a_cum[:, :, -1:] - a_cum), xdt)
    chunk_decay = jnp.exp(a_cum[:, :, -1])

    def step(h, inp):
        dec, st = inp
        return dec[..., None, None] * h + st, h

    h_init = h0.reshape(bsz, SSM_GROUPS, e, SSM_HEAD_DIM, SSM_STATE)
    h_last, h_prev = lax.scan(step, h_init, (jnp.moveaxis(chunk_decay, 1, 0), jnp.moveaxis(states, 1, 0)))
    h_last = h_last.reshape(bsz, SSM_HEADS, SSM_HEAD_DIM, SSM_STATE)
    if not with_y:
        return None, h_last
    h_prev = jnp.moveaxis(h_prev, 0, 1)
    a_t = jnp.moveaxis(a_cum, 2, -1)
    seg = a_t[..., :, None] - a_t[..., None, :]
    causal = jnp.tril(jnp.ones((SSM_CHUNK, SSM_CHUNK), dtype=bool))
    decay_in = jnp.exp(jnp.where(causal, seg, -jnp.inf))
    cb = jnp.einsum('bclgn,bcsgn->bcgls', cc, bc)
    y = (jnp.einsum('bcgls,bcgels,bcsgep->bclgep', cb, decay_in, xdt)
         + jnp.einsum('bclgn,bcgepn,bclge->bclgep', cc, h_prev, jnp.exp(a_cum)))
    return y.reshape(bsz, n, SSM_HEADS, SSM_HEAD_DIM), h_last


def ssd_bidir(pp, conv_w, conv_b, a, dt_bias, d_skip, hf0, hb0, with_y):
    bsz, n, _ = pp.shape
    xbc = jax.nn.silu(short_conv(pp[..., :D_XBC], conv_w, conv_b)).astype(jnp.float32)
    n_bc = SSM_GROUPS * SSM_STATE
    xs = xbc[..., :D_SSM].reshape(bsz, n, SSM_HEADS, SSM_HEAD_DIM)
    bm = xbc[..., D_SSM:D_SSM + n_bc].reshape(bsz, n, SSM_GROUPS, SSM_STATE)
    cm = xbc[..., D_SSM + n_bc:].reshape(bsz, n, SSM_GROUPS, SSM_STATE)
    dt = jax.nn.softplus(pp[..., D_XBC:].astype(jnp.float32).reshape(bsz, n, 2, SSM_HEADS)
                         + dt_bias.astype(jnp.float32))
    y_f, s_f = ssd_scan(xs, dt[:, :, 0], a[0], bm, cm, hf0, with_y)
    y_b, s_b = ssd_scan(flip1(xs), flip1(dt[:, :, 1]), a[1], flip1(bm), flip1(cm), hb0, with_y)
    if not with_y:
        return None, s_f, s_b
    y = y_f + flip1(y_b) + d_skip.astype(jnp.float32)[:, None] * xs
    return y.reshape(bsz, n, D_SSM), s_f, s_b


def ssd_gate_norm(y, z, norm_w):
    return rmsnorm(y * jax.nn.silu(z.astype(jnp.float32)), norm_w).astype(z.dtype)


def ssd_mixer(p_c, p_l, z_c, z_l, conv_w, conv_b, a_log, dt_bias, d_skip, norm_w, ctx_out):
    bsz = p_l.shape[0]
    a = -jnp.exp(a_log.astype(jnp.float32))
    h0 = jnp.zeros((bsz, SSM_HEADS, SSM_HEAD_DIM, SSM_STATE), jnp.float32)
    y_c, sf_c, sb_c = ssd_bidir(p_c, conv_w, conv_b, a, dt_bias, d_skip, h0, h0, ctx_out)
    y_l, _, _ = ssd_bidir(to_col_major(p_l), conv_w, conv_b, a, dt_bias, d_skip, sf_c, sb_c, True)
    out_l = ssd_gate_norm(to_row_major(y_l), z_l, norm_w)
    out_c = ssd_gate_norm(y_c, z_c, norm_w) if ctx_out else None
    return out_c, out_l


def merge_branches(gate_logits, y_r, y_h, y_s, w_branch, w_out):
    g = jax.nn.sigmoid(gate_logits.astype(jnp.float32)).astype(y_r.dtype)
    g_r, g_h, g_s = jnp.split(g, N_BRANCH, axis=-1)
    m = g_r * (y_r @ w_branch[0]) + g_h * (y_h @ w_branch[1]) + g_s * (y_s @ w_branch[2])
    return m @ w_out


def swiglu(h, w_up, w_down):
    g, u = jnp.split(h @ w_up, 2, axis=-1)
    return (jax.nn.silu(g) * u) @ w_down


def setup_inputs(seed: int = 0) -> dict:
    key = jax.random.key(seed)
    ks = iter(list(jax.random.split(key, 48)))
    f32 = jnp.float32

    def nrm(shape, scale):
        return jax.random.normal(next(ks), shape, f32) * scale

    x = nrm((BATCH, SEQ, D_MODEL), 1.0)
    c = nrm((BATCH, D_MODEL), 1.0)
    ctx = nrm((BATCH, CTX_LEN, D_MODEL), 1.0)
    c_ctx = nrm((D_MODEL,), 1.0)
    w_mod = nrm((DEPTH, D_MODEL, 6 * D_MODEL), 0.5 * D_MODEL ** -0.5)
    b_mod = nrm((DEPTH, 6 * D_MODEL), 0.02)
    norm_mix = 1.0 + nrm((DEPTH, D_MODEL), 0.02)
    norm_ffn = 1.0 + nrm((DEPTH, D_MODEL), 0.02)
    w_in = nrm((DEPTH, D_MODEL, D_IN), D_MODEL ** -0.5)
    rnn_conv_w = nrm((DEPTH, RNN_CONV, D_RNN), RNN_CONV ** -0.5)
    rnn_conv_b = nrm((DEPTH, D_RNN), 0.02)
    rnn_gate_a_w = nrm((DEPTH, 2, LRU_BLOCKS, LRU_BLOCK_W, LRU_BLOCK_W), LRU_BLOCK_W ** -0.5)
    rnn_gate_a_b = nrm((DEPTH, 2, D_RNN), 0.02)
    rnn_gate_x_w = nrm((DEPTH, 2, LRU_BLOCKS, LRU_BLOCK_W, LRU_BLOCK_W), LRU_BLOCK_W ** -0.5)
    rnn_gate_x_b = nrm((DEPTH, 2, D_RNN), 0.02)
    a0 = jax.random.uniform(next(ks), (DEPTH, 2, D_RNN), f32, 0.9, 0.999) ** (1.0 / LRU_C)
    rnn_lambda = jnp.log(a0) - jnp.log1p(-a0)
    hy_short_w = nrm((DEPTH, HYENA_SHORT, (HYENA_ORDER + 1) * D_HYENA), HYENA_SHORT ** -0.5)
    hy_short_b = nrm((DEPTH, (HYENA_ORDER + 1) * D_HYENA), 0.02)
    hy_w1 = nrm((DEPTH, HYENA_EMB, HYENA_FFN), HYENA_EMB ** -0.5)
    hy_b1 = nrm((DEPTH, HYENA_FFN), 0.1)
    hy_w2 = nrm((DEPTH, HYENA_FFN, HYENA_FFN), HYENA_FFN ** -0.5)
    hy_b2 = nrm((DEPTH, HYENA_FFN), 0.1)
    hy_w3 = nrm((DEPTH, HYENA_FFN, 2 * HYENA_ORDER * D_HYENA), HYENA_FFN ** -0.5)
    hy_freq = 1.0 + nrm((DEPTH, HYENA_FFN), 0.1)
    hy_decay = jax.random.uniform(next(ks), (DEPTH, 2 * HYENA_ORDER * D_HYENA), f32, HYENA_MIN_DECAY, HYENA_MAX_DECAY)
    hy_bias = nrm((DEPTH, HYENA_ORDER, D_HYENA), 1.0)
    ssm_conv_w = nrm((DEPTH, SSM_CONV, D_XBC), SSM_CONV ** -0.5)
    ssm_conv_b = nrm((DEPTH, D_XBC), 0.02)
    ssm_a_log = jnp.log(jax.random.uniform(next(ks), (DEPTH, 2, SSM_HEADS), f32, 1.0, 16.0))
    dt0 = jnp.exp(jax.random.uniform(next(ks), (DEPTH, 2, SSM_HEADS), f32, math.log(1e-3), math.log(1e-1)))
    ssm_dt_bias = dt0 + jnp.log(-jnp.expm1(-dt0))
    ssm_d = 1.0 + nrm((DEPTH, SSM_HEADS), 0.1)
    ssm_norm = 1.0 + nrm((DEPTH, D_SSM), 0.02)
    w_branch = nrm((DEPTH, N_BRANCH, D_BRANCH, D_MODEL), D_BRANCH ** -0.5)
    w_out = nrm((DEPTH, D_MODEL, D_MODEL), D_MODEL ** -0.5)
    w_up = nrm((DEPTH, D_MODEL, 2 * D_FF), D_MODEL ** -0.5)
    w_down = nrm((DEPTH, D_FF, D_MODEL), D_FF ** -0.5)
    final_norm = 1.0 + nrm((D_MODEL,), 0.02)
    return {'x': x, 'c': c, 'ctx': ctx, 'c_ctx': c_ctx, 'w_mod': w_mod, 'b_mod': b_mod,
            'norm_mix': norm_mix, 'norm_ffn': norm_ffn, 'w_in': w_in,
            'rnn_conv_w': rnn_conv_w, 'rnn_conv_b': rnn_conv_b,
            'rnn_gate_a_w': rnn_gate_a_w, 'rnn_gate_a_b': rnn_gate_a_b,
            'rnn_gate_x_w': rnn_gate_x_w, 'rnn_gate_x_b': rnn_gate_x_b, 'rnn_lambda': rnn_lambda,
            'hy_short_w': hy_short_w, 'hy_short_b': hy_short_b, 'hy_w1': hy_w1, 'hy_b1': hy_b1,
            'hy_w2': hy_w2, 'hy_b2': hy_b2, 'hy_w3': hy_w3, 'hy_freq': hy_freq,
            'hy_decay': hy_decay, 'hy_bias': hy_bias,
            'ssm_conv_w': ssm_conv_w, 'ssm_conv_b': ssm_conv_b, 'ssm_a_log': ssm_a_log,
            'ssm_dt_bias': ssm_dt_bias, 'ssm_d': ssm_d, 'ssm_norm': ssm_norm,
            'w_branch': w_branch, 'w_out': w_out, 'w_up': w_up, 'w_down': w_down,
            'final_norm': final_norm}


def reference(x, c, ctx, c_ctx, w_mod, b_mod, norm_mix, norm_ffn, w_in,
              rnn_conv_w, rnn_conv_b, rnn_gate_a_w, rnn_gate_a_b, rnn_gate_x_w, rnn_gate_x_b, rnn_lambda,
              hy_short_w, hy_short_b, hy_w1, hy_b1, hy_w2, hy_b2, hy_w3, hy_freq, hy_decay, hy_bias,
              ssm_conv_w, ssm_conv_b, ssm_a_log, ssm_dt_bias, ssm_d, ssm_norm,
              w_branch, w_out, w_up, w_down, final_norm):
    s = ctx
    for i in range(DEPTH):
        ctx_out = i < DEPTH - 1
        mod_l = (jax.nn.silu(c) @ w_mod[i] + b_mod[i])[:, None, :]
        mod_c = jax.nn.silu(c_ctx) @ w_mod[i] + b_mod[i]
        sh1_l, sc1_l, g1_l, sh2_l, sc2_l, g2_l = jnp.split(mod_l, 6, axis=-1)
        sh1_c, sc1_c, g1_c, sh2_c, sc2_c, g2_c = jnp.split(mod_c, 6, axis=-1)

        h_l = rmsnorm(x, norm_mix[i]) * (1 + sc1_l) + sh1_l
        h_c = rmsnorm(s, norm_mix[i]) * (1 + sc1_c) + sh1_c
        rx_l, rg_l, hy_l, sz_l, sp_l, gt_l = split_in(h_l @ w_in[i])
        rx_c, rg_c, hy_c, sz_c, sp_c, gt_c = split_in(h_c @ w_in[i])

        yr_c, yr_l = rglru_mixer(rx_c, rx_l, rg_c, rg_l, rnn_conv_w[i], rnn_conv_b[i],
                                 rnn_gate_a_w[i], rnn_gate_a_b[i], rnn_gate_x_w[i], rnn_gate_x_b[i],
                                 rnn_lambda[i], ctx_out)
        ys_c, ys_l = ssd_mixer(sp_c, sp_l, sz_c, sz_l, ssm_conv_w[i], ssm_conv_b[i], ssm_a_log[i],
                               ssm_dt_bias[i], ssm_d[i], ssm_norm[i], ctx_out)
        yh_l = hyena_mixer(hy_l, hy_short_w[i], hy_short_b[i], hy_w1[i], hy_b1[i], hy_w2[i], hy_b2[i],
                           hy_w3[i], hy_freq[i], hy_decay[i], hy_bias[i])
        x = x + g1_l * merge_branches(gt_l, yr_l, yh_l, ys_l, w_branch[i], w_out[i])
        f_l = rmsnorm(x, norm_ffn[i]) * (1 + sc2_l) + sh2_l
        x = x + g2_l * swiglu(f_l, w_up[i], w_down[i])

        if ctx_out:
            yh_c = hyena_mixer(hy_c, hy_short_w[i], hy_short_b[i], hy_w1[i], hy_b1[i], hy_w2[i], hy_b2[i],
                               hy_w3[i], hy_freq[i], hy_decay[i], hy_bias[i])
            s = s + g1_c * merge_branches(gt_c, yr_c, yh_c, ys_c, w_branch[i], w_out[i])
            f_c = rmsnorm(s, norm_ffn[i]) * (1 + sc2_c) + sh2_c
            s = s + g2_c * swiglu(f_c, w_up[i], w_down[i])
    return rmsnorm(x, final_norm)
```

```python
import functools
import math

import numpy as np
import jax
import jax.numpy as jnp
from jax import lax
from jax.experimental import pallas as pl
from jax.experimental.pallas import tpu as pltpu

F32 = jnp.float32
BF16 = jnp.bfloat16
HI = lax.Precision.HIGHEST

NB = 8
GRID_W = 64
NORM_EPS = 1e-6
LRU_C = 8.0
LRU_BW = 128
SSM_P = 64
SSM_H = 16
SSM_G = 2
SSM_N = 128
SSM_Q = 128
HY_BANDS = 16
HY_PAD = 128

D = 1024
OFF_RX, OFF_RG, OFF_HY, OFF_SZ, OFF_GT, OFF_XBC, OFF_DT = 0, 1024, 2048, 5120, 6144, 9216, 10752
D_XBC = 1536
D_INP = 11264

TM = 512


VMEM_LIMIT = 56 * 1024 * 1024


def _cp(*sem):
    return pltpu.CompilerParams(dimension_semantics=sem, vmem_limit_bytes=VMEM_LIMIT)


def _bdot(a, b):
    return jnp.dot(a, b, preferred_element_type=F32)


def _mod_rows(y, scale8, shift8):
    tm, d = y.shape
    y3 = y.reshape(tm // NB, NB, d)
    if scale8 is not None:
        y3 = y3 * scale8[None]
    if shift8 is not None:
        y3 = y3 + shift8[None]
    return y3.reshape(tm, d)


def _rms(x, g):
    ms = jnp.mean(x * x, axis=-1, keepdims=True)
    return x * lax.rsqrt(ms + NORM_EPS) * g


def _mod_kernel(c_ref, w_ref, b_ref, o_ref):
    c = c_ref[...]
    s = c * jax.nn.sigmoid(c)
    o_ref[0] = jnp.dot(s, w_ref[0], precision=HI, preferred_element_type=F32) + b_ref[0]


def modulation(c16, w_mod, b_mod):
    nl, d, n = w_mod.shape
    tn = 1536
    return pl.pallas_call(
        _mod_kernel,
        grid=(nl, n // tn),
        in_specs=[pl.BlockSpec((16, d), lambda l, j: (0, 0)),
                  pl.BlockSpec((1, d, tn), lambda l, j: (l, 0, j)),
                  pl.BlockSpec((1, 1, tn), lambda l, j: (l, 0, j))],
        out_specs=pl.BlockSpec((1, 16, tn), lambda l, j: (l, 0, j)),
        out_shape=jax.ShapeDtypeStruct((nl, 16, n), F32),
        compiler_params=_cp("parallel", "parallel"),
        name="modulation",
    )(c16, w_mod, b_mod.reshape(nl, 1, n))


def _inproj_kernel(x_ref, g_ref, sh_ref, sc_ref, w_ref, o_ref, xn_ref):
    @pl.when(pl.program_id(1) == 0)
    def _():
        y = _rms(x_ref[...], g_ref[...])
        xn_ref[...] = _mod_rows(y, 1.0 + sc_ref[...], sh_ref[...]).astype(BF16)

    o_ref[...] = _bdot(xn_ref[...], w_ref[...])


def in_proj(x, g, mod, mrow, w):
    m, d = x.shape
    n = w.shape[1]
    tn = 1024
    return pl.pallas_call(
        _inproj_kernel,
        grid=(m // TM, n // tn),
        in_specs=[pl.BlockSpec((TM, d), lambda i, j: (i, 0)),
                  pl.BlockSpec((1, d), lambda i, j: (0, 0)),
                  pl.BlockSpec((NB, d), lambda i, j: (mrow, 0)),
                  pl.BlockSpec((NB, d), lambda i, j: (mrow, 1)),
                  pl.BlockSpec((d, tn), lambda i, j: (0, j))],
        out_specs=pl.BlockSpec((TM, tn), lambda i, j: (i, j)),
        out_shape=jax.ShapeDtypeStruct((m, n), F32),
        scratch_shapes=[pltpu.VMEM((TM, d), BF16)],
        compiler_params=_cp("parallel", "arbitrary"),
        name="in_proj",
    )(x, g, mod, mod, w)


def _expm1_neg(x):
    poly = x * (1.0 + x * (0.5 + x * (1.0 / 6 + x * (1.0 / 24 + x * (1.0 / 120 + x * (1.0 / 720))))))
    return jnp.where(x > -0.1, poly, jnp.exp(x) - 1.0)


def _rglru_kernel(*refs, rev, final, tt, cb, n_t):
    if final:
        (xm, xp, xn, rg, hf, cw, cbias, wa, wx, ba, bx, lam, h0, out, hlast, ext_s, a_s, g_s, h_s) = refs
    else:
        (xm, xp, xn, cw, cbias, wa, wx, ba, bx, lam, h0, out, hlast, ext_s, a_s, g_s, h_s) = refs
    r = tt * NB
    i = pl.program_id(1)
    ti = (n_t - 1 - i) if rev else i

    @pl.when(i == 0)
    def _():
        h_s[...] = h0[...]

    ext_s[0:16, :] = jnp.where(ti == 0, 0.0, xp[...])
    ext_s[16:16 + r, :] = xm[...]
    ext_s[16 + r:24 + r, :] = jnp.where(ti == n_t - 1, 0.0, xn[...])
    lam_v = lam[...]
    sp = jnp.maximum(-lam_v, 0.0) + jnp.log(1.0 + jnp.exp(-jnp.abs(lam_v)))
    sub = 128
    for rc in range(r // sub):
        r0 = rc * sub
        for k in range(cb // LRU_BW):
            ls = slice(k * LRU_BW, (k + 1) * LRU_BW)
            u = cbias[:, ls]
            for j in range(4):
                u = u + cw[j:j + 1, ls] * ext_s[r0 + 8 * j:r0 + 8 * j + sub, ls]
            ub = u.astype(BF16)
            rr = jax.nn.sigmoid(_bdot(ub, wa[k]) + ba[:, ls])
            ii = jax.nn.sigmoid(_bdot(ub, wx[k]) + bx[:, ls])
            la = (-LRU_C) * rr * sp[:, ls]
            a_s[r0:r0 + sub, ls] = jnp.exp(la)
            g_s[r0:r0 + sub, ls] = jnp.sqrt(-_expm1_neg(2.0 * la)) * (ii * u)

    def step(s, h):
        t = (tt - 1 - s) if rev else s
        rows = pl.ds(pl.multiple_of(t * NB, NB), NB)
        h = a_s[rows, :] * h + g_s[rows, :]
        g_s[rows, :] = h
        return h

    h = lax.fori_loop(0, tt, step, h_s[...], unroll=8)
    h_s[...] = h
    hlast[...] = h
    if final:
        for rc in range(r // sub):
            rs = slice(rc * sub, (rc + 1) * sub)
            out[rs, :] = ((hf[rs, :] + g_s[rs, :]) * jax.nn.gelu(rg[rs, :])).astype(out.dtype)
    else:
        out[...] = g_s[...]


def rglru_pass(proj, n, d_idx, rev, h0, p, hf=None):
    final = hf is not None
    tt = min(64, n)
    cb = 256
    r = tt * NB
    n_t = n // tt
    nc = D // cb

    def tix(i):
        return (n_t - 1 - i) if rev else i

    main = pl.BlockSpec((r, cb), lambda c, i: (tix(i), c))
    prev = pl.BlockSpec((16, cb), lambda c, i: (jnp.maximum(tix(i) * (r // 16) - 1, 0), c))
    nxt = pl.BlockSpec((NB, cb), lambda c, i: (jnp.minimum((tix(i) + 1) * tt, n - 1), c))
    vec = pl.BlockSpec((1, cb), lambda c, i: (0, c))
    wsp = pl.BlockSpec((cb // LRU_BW, LRU_BW, LRU_BW), lambda c, i: (c, 0, 0))
    in_specs = [main, prev, nxt]
    args = [proj, proj, proj]
    if final:
        in_specs += [pl.BlockSpec((r, cb), lambda c, i: (tix(i), OFF_RG // cb + c)), main]
        args += [proj, hf]
    in_specs += [pl.BlockSpec((4, cb), lambda c, i: (0, c)), vec, wsp, wsp, vec, vec, vec,
                 pl.BlockSpec((NB, cb), lambda c, i: (0, c))]
    args += [p["conv_w"], p["conv_b"], p["wa"][d_idx], p["wx"][d_idx], p["ba"][d_idx], p["bx"][d_idx],
             p["lam"][d_idx], h0]
    out, hlast = pl.pallas_call(
        functools.partial(_rglru_kernel, rev=rev, final=final, tt=tt, cb=cb, n_t=n_t),
        grid=(nc, n_t),
        in_specs=in_specs,
        out_specs=[main, pl.BlockSpec((NB, cb), lambda c, i: (0, c))],
        out_shape=[jax.ShapeDtypeStruct((n * NB, D), BF16 if final else F32),
                   jax.ShapeDtypeStruct((NB, D), F32)],
        scratch_shapes=[pltpu.VMEM((r + 24, cb), F32), pltpu.VMEM((r, cb), F32), pltpu.VMEM((r, cb), F32),
                        pltpu.VMEM((NB, cb), F32)],
        compiler_params=_cp("parallel", "arbitrary"),
        name="rglru_bwd" if rev else "rglru_fwd",
    )(*args)
    return out, hlast


def _hyshort_kernel(xm, xp, xn, w, b, o, *, n_t):
    i = pl.program_id(0)
    r = xm.shape[0]
    prev = jnp.where(i == 0, 0.0, xp[...])
    nxt = jnp.where(i == n_t - 1, 0.0, xn[...])
    x = xm[...]
    sh_prev = jnp.concatenate([prev, x[:r - NB]], axis=0)
    sh_next = jnp.concatenate([x[NB:], nxt], axis=0)
    o[...] = w[0:1, :] * sh_prev + w[1:2, :] * x + w[2:3, :] * sh_next + b[...]


def hyena_short(proj, n, w, b):
    tt = min(32, n)
    r = tt * NB
    n_t = n // tt
    cb = 512
    c0 = OFF_HY // cb
    return pl.pallas_call(
        functools.partial(_hyshort_kernel, n_t=n_t),
        grid=(n_t, 3 * D // cb),
        in_specs=[pl.BlockSpec((r, cb), lambda i, c: (i, c0 + c)),
                  pl.BlockSpec((NB, cb), lambda i, c: (jnp.maximum(i * tt - 1, 0), c0 + c)),
                  pl.BlockSpec((NB, cb), lambda i, c: (jnp.minimum((i + 1) * tt, n - 1), c0 + c)),
                  pl.BlockSpec((3, cb), lambda i, c: (0, c)),
                  pl.BlockSpec((1, cb), lambda i, c: (0, c))],
        out_specs=pl.BlockSpec((r, cb), lambda i, c: (i, c)),
        out_shape=jax.ShapeDtypeStruct((n * NB, 3 * D), F32),
        compiler_params=_cp("parallel", "parallel"),
        name="hyena_short",
    )(proj, proj, proj, w, b)


def _hyfilt_kernel(emb, w1, b1, w2, b2, fr, w3f, w3b, dcf, dcb, of, ob):
    n = emb.shape[0]
    f = fr[...]
    h = jnp.sin(f * (jnp.dot(emb[...], w1[...], precision=HI, preferred_element_type=F32) + b1[...]))
    h = jnp.sin(f * (jnp.dot(h, w2[...], precision=HI, preferred_element_type=F32) + b2[...]))
    row = lax.broadcasted_iota(jnp.int32, (n, 1), 0)
    t = row.astype(F32) * (1.0 / (n - 1))
    kf = jnp.dot(h, w3f[...], precision=HI, preferred_element_type=F32) * jnp.exp(-t * dcf[...])
    kb = jnp.dot(h, w3b[...], precision=HI, preferred_element_type=F32) * jnp.exp(-t * dcb[...])
    kb = jnp.where(row == 0, 0.0, kb)
    nrm = jnp.sum(jnp.abs(kf), axis=0, keepdims=True) + jnp.sum(jnp.abs(kb), axis=0, keepdims=True)
    inv = 1.0 / nrm
    of[...] = (kf * inv).astype(of.dtype)
    ob[...] = (kb * inv).astype(ob.dtype)


def hyena_filters(emb, p):
    n = emb.shape[0]
    tn = 256
    nj = 2 * D // tn
    full = lambda s: pl.BlockSpec(s, lambda j: (0, 0))
    return pl.pallas_call(
        _hyfilt_kernel,
        grid=(nj,),
        in_specs=[full((n, HY_PAD)), full((HY_PAD, HY_PAD)), full((1, HY_PAD)), full((HY_PAD, HY_PAD)),
                  full((1, HY_PAD)), full((1, HY_PAD)),
                  pl.BlockSpec((HY_PAD, tn), lambda j: (0, j)), pl.BlockSpec((HY_PAD, tn), lambda j: (0, nj + j)),
                  pl.BlockSpec((1, tn), lambda j: (0, j)), pl.BlockSpec((1, tn), lambda j: (0, nj + j))],
        out_specs=[pl.BlockSpec((n, tn), lambda j: (0, j)), pl.BlockSpec((n, tn), lambda j: (0, j))],
        out_shape=[jax.ShapeDtypeStruct((n, 2 * D), F32), jax.ShapeDtypeStruct((n, 2 * D), F32)],
        compiler_params=_cp("parallel"),
        name="hyena_filters",
    )(emb, p["w1"], p["b1"], p["w2"], p["b2"], p["freq"], p["w3"], p["w3"], p["decay"], p["decay"])


def _dftp_kernel(c_ref, s_ref, x_ref, a_ref, b_ref, acc_a, acc_b, *, nk):
    k = pl.program_id(2)

    @pl.when(k == 0)
    def _():
        acc_a[...] = jnp.zeros_like(acc_a)
        acc_b[...] = jnp.zeros_like(acc_b)

    xb = x_ref[...].astype(BF16)
    acc_a[...] += _bdot(c_ref[...], xb)
    acc_b[...] += _bdot(s_ref[...], xb)

    @pl.when(k == nk - 1)
    def _():
        a_ref[...] = acc_a[...]
        b_ref[...] = acc_b[...]


def _kcombine_kernel(af_ref, ab_ref, bf_ref, bb_ref, kre_ref, kim_ref):
    i = pl.program_id(0)
    kre_ref[...] = af_ref[...] + ab_ref[...]
    bf, bb = bf_ref[...], bb_ref[...]
    first = (lax.broadcasted_iota(jnp.int32, bf.shape, 0) == 0) & (i == 0)
    kim_ref[...] = jnp.where(first, bf + bb, bb - bf)


def _dft_tiles(n):
    return min(1024, n), min(512, n)


def hyena_kspec(tabs, kf, kb):
    cmat, smat, _ = tabs
    n = cmat.shape[0]
    tf, tk = _dft_tiles(n)
    nk = n // tk
    x = jnp.concatenate([kf, kb], axis=1)
    nj = x.shape[1] // D
    tab = pl.BlockSpec((tf, tk), lambda i, j, k: (i, k))
    out = pl.BlockSpec((tf, D), lambda i, j, k: (i, j))
    a, b = pl.pallas_call(
        functools.partial(_dftp_kernel, nk=nk),
        grid=(n // tf, nj, nk),
        in_specs=[tab, tab, pl.BlockSpec((tk, D), lambda i, j, k: (k, j))],
        out_specs=[out, out],
        out_shape=[jax.ShapeDtypeStruct(x.shape, F32), jax.ShapeDtypeStruct(x.shape, F32)],
        scratch_shapes=[pltpu.VMEM((tf, D), F32), pltpu.VMEM((tf, D), F32)],
        compiler_params=_cp("parallel", "parallel", "arbitrary"),
        name="hyena_kdft",
    )(cmat, smat, x)
    fwd = pl.BlockSpec((tf, D), lambda i, j: (i, j))
    bwd = pl.BlockSpec((tf, D), lambda i, j: (i, nj // 2 + j))
    return pl.pallas_call(
        _kcombine_kernel,
        grid=(n // tf, nj // 2),
        in_specs=[fwd, bwd, fwd, bwd],
        out_specs=[fwd, fwd],
        out_shape=[jax.ShapeDtypeStruct((n, 2 * D), F32), jax.ShapeDtypeStruct((n, 2 * D), F32)],
        compiler_params=_cp("parallel", "parallel"),
        name="hyena_kcombine",
    )(a, a, b, b)


def _dftf_kernel(c_ref, s_ref, x_ref, kre_ref, kim_ref, qa_ref, qb_ref, acc_a, acc_b, *, nk, inv_n):
    i = pl.program_id(0)
    k = pl.program_id(2)

    @pl.when(k == 0)
    def _():
        acc_a[...] = jnp.zeros_like(acc_a)
        acc_b[...] = jnp.zeros_like(acc_b)

    xb = x_ref[...].astype(BF16)
    acc_a[...] += _bdot(c_ref[...], xb)
    acc_b[...] += _bdot(s_ref[...], xb)

    @pl.when(k == nk - 1)
    def _():
        a, b, kre, kim = acc_a[...], acc_b[...], kre_ref[...], kim_ref[...]
        first = (lax.broadcasted_iota(jnp.int32, a.shape, 0) == 0) & (i == 0)
        qa = jnp.where(first, inv_n * (a * kre), (2.0 * inv_n) * (a * kre + b * kim))
        qb = jnp.where(first, inv_n * (b * kim), (2.0 * inv_n) * (b * kre - a * kim))
        qa_ref[...] = qa.astype(qa_ref.dtype)
        qb_ref[...] = qb.astype(qb_ref.dtype)


def hyena_fwd(tabs, x2d, xcol, kre, kim, order):
    cmat, smat, _ = tabs
    n = cmat.shape[0]
    tf, tk = _dft_tiles(n)
    nk = n // tk
    return pl.pallas_call(
        functools.partial(_dftf_kernel, nk=nk, inv_n=1.0 / (2 * n)),
        grid=(n // tf, NB, nk),
        in_specs=[pl.BlockSpec((tf, tk), lambda i, j, k: (i, k)), pl.BlockSpec((tf, tk), lambda i, j, k: (i, k)),
                  pl.BlockSpec((tk, D), lambda i, j, k: (k, xcol(j))),
                  pl.BlockSpec((tf, D), lambda i, j, k: (i, order)), pl.BlockSpec((tf, D), lambda i, j, k: (i, order))],
        out_specs=[pl.BlockSpec((tf, D), lambda i, j, k: (i, j)), pl.BlockSpec((tf, D), lambda i, j, k: (i, j))],
        out_shape=[jax.ShapeDtypeStruct((n, NB * D), BF16), jax.ShapeDtypeStruct((n, NB * D), BF16)],
        scratch_shapes=[pltpu.VMEM((tf, D), F32)] * 2,
        compiler_params=_cp("parallel", "parallel", "arbitrary"),
        name="hyena_fwd",
    )(cmat, smat, x2d, kre, kim)


def _dfti_kernel(c_ref, st_ref, qa_ref, qb_ref, xk_ref, z_ref, bias_ref, o_ref, acc, *, nk):
    k = pl.program_id(2)

    @pl.when(k == 0)
    def _():
        acc[...] = jnp.zeros_like(acc)

    acc[...] += _bdot(c_ref[...], qa_ref[...]) + _bdot(st_ref[...], qb_ref[...])

    @pl.when(k == nk - 1)
    def _():
        o_ref[...] = (xk_ref[...] * (acc[...] + bias_ref[0] * z_ref[...])).astype(o_ref.dtype)


def hyena_inv(tabs, qa, qb, u2d, xkcol, z2d, zcol, bias, order, out_dtype):
    cmat, _, smat_t = tabs
    n = cmat.shape[0]
    tf, tk = _dft_tiles(n)
    nk = n // tk
    return pl.pallas_call(
        functools.partial(_dfti_kernel, nk=nk),
        grid=(n // tf, NB, nk),
        in_specs=[pl.BlockSpec((tf, tk), lambda i, j, k: (i, k)), pl.BlockSpec((tf, tk), lambda i, j, k: (i, k)),
                  pl.BlockSpec((tk, D), lambda i, j, k: (k, j)), pl.BlockSpec((tk, D), lambda i, j, k: (k, j)),
                  pl.BlockSpec((tf, D), lambda i, j, k: (i, xkcol(j))),
                  pl.BlockSpec((tf, D), lambda i, j, k: (i, zcol(j))),
                  pl.BlockSpec((1, 1, D), lambda i, j, k: (order, 0, 0))],
        out_specs=pl.BlockSpec((tf, D), lambda i, j, k: (i, j)),
        out_shape=jax.ShapeDtypeStruct((n, NB * D), out_dtype),
        scratch_shapes=[pltpu.VMEM((tf, D), F32)],
        compiler_params=_cp("parallel", "parallel", "arbitrary"),
        name="hyena_inv",
    )(cmat, smat_t, qa, qb, u2d, z2d, bias)


DFT_R = 64


def _dft_table_kernel(ac_ref, as_ref, bc_ref, bs_ref, c_ref, s_ref, st_ref):
    f1 = pl.program_id(0)
    ac, as_ = ac_ref[0], as_ref[0]
    bc, bs = bc_ref[...], bs_ref[...]
    c = ac * bc - as_ * bs
    s = as_ * bc + ac * bs
    row = lax.broadcasted_iota(jnp.int32, c.shape, 0)
    col = lax.broadcasted_iota(jnp.int32, c.shape, 1)
    alt_col = jnp.where(col % 2 == 0, 1.0, -1.0)
    alt_row = jnp.where(row % 2 == 0, 1.0, -1.0)
    c_ref[...] = c.astype(c_ref.dtype)
    s_ref[...] = jnp.where((row == 0) & (f1 == 0), alt_col, s).astype(s_ref.dtype)
    st_ref[...] = jnp.where(col == 0, alt_row, s).astype(st_ref.dtype)


def dft_tables(n):
    r = min(DFT_R, n)
    t = np.arange(n, dtype=np.int64)[None, :]
    f1 = np.arange(n // r, dtype=np.int64)[:, None]
    f0 = np.arange(r, dtype=np.int64)[:, None]
    ang_a = ((r * f1 * t) % (2 * n)).astype(np.float64) * (np.pi / n)
    ang_b = ((f0 * t) % (2 * n)).astype(np.float64) * (np.pi / n)
    ac = jnp.asarray(np.cos(ang_a), F32).reshape(n // r, 1, n)
    as_ = jnp.asarray(np.sin(ang_a), F32).reshape(n // r, 1, n)
    bc = jnp.asarray(np.cos(ang_b), F32)
    bs = jnp.asarray(np.sin(ang_b), F32)
    arow = pl.BlockSpec((1, 1, n), lambda i: (i, 0, 0))
    bfull = pl.BlockSpec((r, n), lambda i: (0, 0))
    out = pl.BlockSpec((r, n), lambda i: (i, 0))
    return tuple(pl.pallas_call(
        _dft_table_kernel,
        grid=(n // r,),
        in_specs=[arow, arow, bfull, bfull],
        out_specs=[out, out, out],
        out_shape=[jax.ShapeDtypeStruct((n, n), BF16)] * 3,
        compiler_params=_cp("parallel"),
        name="dft_tables",
    )(ac, as_, bc, bs))


def hyena_embedding(n):
    t = np.linspace(0.0, 1.0, n)[:, None]
    bands = np.linspace(1e-4, HY_BANDS - 1, HY_BANDS)
    ang = (2.0 * math.pi / n) * np.arange(n, dtype=np.float64)[:, None] * bands[None]
    emb = np.concatenate([t, np.cos(ang), np.sin(ang)], axis=-1)
    return jnp.asarray(np.pad(emb, ((0, 0), (0, HY_PAD - emb.shape[1]))), F32)


def hyena_mixer(proj, n, tabs, emb, p):
    u = hyena_short(proj, n, p["short_w"], p["short_b"])
    kf, kb = hyena_filters(emb, p)
    kre, kim = hyena_kspec(tabs, kf, kb)
    u2d = u.reshape(n, NB * 3 * D)
    qa, qb = hyena_fwd(tabs, u2d, lambda b: 3 * b, kre, kim, 0)
    z2 = hyena_inv(tabs, qa, qb, u2d, lambda b: 3 * b + 1, u2d, lambda b: 3 * b, p["bias"], 0, F32)
    qa, qb = hyena_fwd(tabs, z2, lambda b: b, kre, kim, 1)
    y = hyena_inv(tabs, qa, qb, u2d, lambda b: 3 * b + 2, z2, lambda b: b, p["bias"], 1, BF16)
    return y.reshape(n * NB, D)


def _softplus(x):
    return jnp.maximum(x, 0.0) + jnp.log(1.0 + jnp.exp(-jnp.abs(x)))


def _ssd_kernel(*refs, rev, nc):
    if rev:
        (xm, xp, xn, dt_ref, yprev, cw, cbias, dtb, arow, xexp, h0, y_ref, hlast, ext_s, s_s) = refs
    else:
        (xm, xp, xn, dt_ref, cw, cbias, dtb, arow, dsk, xexp, h0, y_ref, hlast, ext_s, s_s) = refs
    q = SSM_Q
    i = pl.program_id(1)
    ci = (nc - 1 - i) if rev else i

    @pl.when(i == 0)
    def _():
        s_s[...] = h0[0]

    ext_s[0:8, :] = jnp.where(ci == 0, 0.0, xp[0])
    ext_s[8:8 + q, :] = xm[0]
    ext_s[8 + q:16 + q, :] = jnp.where(ci == nc - 1, 0.0, xn[0])
    acc = cbias[...]
    for j in range(4):
        acc = acc + cw[j:j + 1, :] * ext_s[pl.ds(6 + j, q), :]
    xbc = acc * jax.nn.sigmoid(acc)
    xs = xbc[:, :D]
    dtp = _softplus(dt_ref[0] + dtb[...])
    dta = dtp * arow[...]
    row = lax.broadcasted_iota(jnp.int32, (q, q), 0)
    col = lax.broadcasted_iota(jnp.int32, (q, q), 1)
    mask = (row <= col) if rev else (row >= col)
    cum = jnp.dot(mask.astype(F32), dta, precision=HI, preferred_element_type=F32)
    cum_t = cum.T
    cum_e = jnp.dot(cum, xexp[...], precision=HI, preferred_element_type=F32)
    dt_e = jnp.dot(dtp, xexp[...], precision=HI, preferred_element_type=F32)
    last = 0 if rev else q - 1
    cum_last = cum_e[last:last + 1, :]
    xdt = xs * dt_e
    xdt_b = xdt.astype(BF16)
    s_old = s_s[...]
    s_old_b = s_old.astype(BF16)
    w_b = (jnp.exp(cum_last - cum_e) * xdt).astype(BF16)
    lane = lax.broadcasted_iota(jnp.int32, (q, 2 * SSM_P), 1)
    gw = D // SSM_G
    e_per_g = SSM_H // SSM_G
    slot0 = SSM_H if rev else 0
    y_cols = []
    s_cols = []
    for g in range(SSM_G):
        bg = xbc[:, D + g * SSM_N:D + (g + 1) * SSM_N].astype(BF16)
        cg = xbc[:, D + SSM_G * SSM_N + g * SSM_N:D + SSM_G * SSM_N + (g + 1) * SSM_N].astype(BF16)
        cbm = lax.dot_general(cg, bg, (((1,), (1,)), ((), ())), preferred_element_type=F32)
        for pr in range(e_per_g // 2):
            c0 = g * gw + pr * 2 * SSM_P
            xpair = xdt[:, c0:c0 + 2 * SSM_P]
            yp = jnp.zeros((q, 2 * SSM_P), F32)
            for hh in range(2):
                slot = slot0 + g * e_per_g + pr * 2 + hh
                seg = cum[:, slot:slot + 1] - cum_t[slot:slot + 1, :]
                dec = jnp.exp(jnp.where(mask, seg, -jnp.inf))
                mh = (cbm * dec).astype(BF16)
                in_head = (lane >= hh * SSM_P) & (lane < (hh + 1) * SSM_P)
                yp = yp + _bdot(mh, jnp.where(in_head, xpair, 0.0).astype(BF16))
            y_cols.append(yp)
        gs = slice(g * gw, (g + 1) * gw)
        y_state = _bdot(cg, s_old_b[:, gs])
        y_cols[-(e_per_g // 2):] = [
            y_cols[-(e_per_g // 2) + pr] + jnp.exp(cum_e[:, g * gw + pr * 2 * SSM_P:g * gw + (pr + 1) * 2 * SSM_P])
            * y_state[:, pr * 2 * SSM_P:(pr + 1) * 2 * SSM_P] for pr in range(e_per_g // 2)]
        s_new = jnp.exp(cum_last[:, gs]) * s_old[:, gs] + lax.dot_general(
            bg, w_b[:, gs], (((0,), (0,)), ((), ())), preferred_element_type=F32)
        s_cols.append(s_new)
    y = jnp.concatenate(y_cols, axis=1)
    s_new = jnp.concatenate(s_cols, axis=1)
    s_s[...] = s_new
    hlast[0] = s_new
    if rev:
        y_ref[0] = y + yprev[0]
    else:
        y_ref[0] = y + dsk[...] * xs


def ssd_pass(xbc, dt, rev, h0, p, yprev=None):
    _, n, _ = xbc.shape
    q = SSM_Q
    nc = n // q
    d_idx = 1 if rev else 0

    def cix(i):
        return (nc - 1 - i) if rev else i

    in_specs = [pl.BlockSpec((1, q, D_XBC), lambda b, i: (b, cix(i), 0)),
                pl.BlockSpec((1, NB, D_XBC), lambda b, i: (b, jnp.maximum(cix(i) * (q // NB) - 1, 0), 0)),
                pl.BlockSpec((1, NB, D_XBC), lambda b, i: (b, jnp.minimum((cix(i) + 1) * (q // NB), n // NB - 1), 0)),
                pl.BlockSpec((1, q, 128), lambda b, i: (b, cix(i), 0))]
    args = [xbc, xbc, xbc, dt]
    ymain = pl.BlockSpec((1, q, D), lambda b, i: (b, cix(i), 0))
    if rev:
        in_specs.append(ymain)
        args.append(yprev)
    full = lambda s: pl.BlockSpec(s, lambda b, i: (0,) * len(s))
    in_specs += [full((4, D_XBC)), full((1, D_XBC)), full((1, 128)), full((1, 128))]
    args += [p["conv_w"], p["conv_b"], p["dt_bias"], p["arow"][d_idx]]
    if not rev:
        in_specs.append(full((1, D)))
        args.append(p["d_skip"])
    in_specs += [full((128, D)), pl.BlockSpec((1, SSM_N, D), lambda b, i: (b, 0, 0))]
    args += [p["xexp"][d_idx], h0]
    y, hlast = pl.pallas_call(
        functools.partial(_ssd_kernel, rev=rev, nc=nc),
        grid=(NB, nc),
        in_specs=in_specs,
        out_specs=[ymain, pl.BlockSpec((1, SSM_N, D), lambda b, i: (b, 0, 0))],
        out_shape=[jax.ShapeDtypeStruct((NB, n, D), F32), jax.ShapeDtypeStruct((NB, SSM_N, D), F32)],
        scratch_shapes=[pltpu.VMEM((q + 16, D_XBC), F32), pltpu.VMEM((SSM_N, D), F32)],
        compiler_params=_cp("parallel", "arbitrary"),
        name="ssd_bwd" if rev else "ssd_fwd",
    )(*args)
    return y, hlast


def ssd_bidir(xbc, dt, hf0, hb0, p):
    yf, sf = ssd_pass(xbc, dt, False, hf0, p)
    y, sb = ssd_pass(xbc, dt, True, hb0, p, yprev=yf)
    return y, sf, sb


def _merge_kernel(x_ref, gt_ref, yr_ref, yh_ref, ys_ref, z_ref, nw_ref, wb_ref, wo_ref, g1_ref, o_ref):
    z = z_ref[...]
    ys = _rms(ys_ref[...] * (z * jax.nn.sigmoid(z)), nw_ref[...])
    m = jax.nn.sigmoid(gt_ref[:, 0:D]) * _bdot(yr_ref[...], wb_ref[0])
    m = m + jax.nn.sigmoid(gt_ref[:, D:2 * D]) * _bdot(yh_ref[...], wb_ref[1])
    m = m + jax.nn.sigmoid(gt_ref[:, 2 * D:3 * D]) * _bdot(ys.astype(BF16), wb_ref[2])
    o = _bdot(m.astype(BF16), wo_ref[...])
    o_ref[...] = x_ref[...] + _mod_rows(o, g1_ref[...], None)


def merge(x, proj, yr, yh, ys, norm_w, wb, wo, mod, mrow):
    m = x.shape[0]
    row = lambda w, c: pl.BlockSpec((TM, w), lambda i: (i, c))
    return pl.pallas_call(
        _merge_kernel,
        grid=(m // TM,),
        in_specs=[row(D, 0), row(3 * D, OFF_GT // (3 * D)), row(D, 0), row(D, 0), row(D, 0), row(D, OFF_SZ // D),
                  pl.BlockSpec((1, D), lambda i: (0, 0)),
                  pl.BlockSpec((3, D, D), lambda i: (0, 0, 0)),
                  pl.BlockSpec((D, D), lambda i: (0, 0)),
                  pl.BlockSpec((NB, D), lambda i: (mrow, 2))],
        out_specs=row(D, 0),
        out_shape=jax.ShapeDtypeStruct((m, D), F32),
        compiler_params=_cp("parallel"),
        name="merge",
    )(x, proj, yr, yh, ys, proj, norm_w, wb, wo, mod)


def _ffn_kernel(x_ref, g_ref, sh_ref, sc_ref, g2_ref, wg_ref, wu_ref, wd_ref, fn_ref, o_ref, xn_s, acc_s, *, nf, final):
    f = pl.program_id(1)

    @pl.when(f == 0)
    def _():
        y = _rms(x_ref[...], g_ref[...])
        xn_s[...] = _mod_rows(y, 1.0 + sc_ref[...], sh_ref[...]).astype(BF16)
        acc_s[...] = jnp.zeros_like(acc_s)

    xn = xn_s[...]
    hg = _bdot(xn, wg_ref[...])
    hu = _bdot(xn, wu_ref[...])
    hh = (hg * jax.nn.sigmoid(hg) * hu).astype(BF16)
    acc_s[...] += _bdot(hh, wd_ref[...])

    @pl.when(f == nf - 1)
    def _():
        o = x_ref[...] + _mod_rows(acc_s[...], g2_ref[...], None)
        if final:
            o = _rms(o, fn_ref[...])
        o_ref[...] = o


def ffn(x, g, mod, mrow, w_up, w_down, final_norm, final):
    m = x.shape[0]
    d_ff = w_down.shape[0]
    nf = 2
    tf = d_ff // nf
    vec = pl.BlockSpec((1, D), lambda i, f: (0, 0))
    return pl.pallas_call(
        functools.partial(_ffn_kernel, nf=nf, final=final),
        grid=(m // TM, nf),
        in_specs=[pl.BlockSpec((TM, D), lambda i, f: (i, 0)), vec,
                  pl.BlockSpec((NB, D), lambda i, f: (mrow, 3)),
                  pl.BlockSpec((NB, D), lambda i, f: (mrow, 4)),
                  pl.BlockSpec((NB, D), lambda i, f: (mrow, 5)),
                  pl.BlockSpec((D, tf), lambda i, f: (0, f)),
                  pl.BlockSpec((D, tf), lambda i, f: (0, nf + f)),
                  pl.BlockSpec((tf, D), lambda i, f: (f, 0)), vec],
        out_specs=pl.BlockSpec((TM, D), lambda i, f: (i, 0)),
        out_shape=jax.ShapeDtypeStruct((m, D), F32),
        scratch_shapes=[pltpu.VMEM((TM, D), BF16), pltpu.VMEM((TM, D), F32)],
        compiler_params=_cp("parallel", "arbitrary"),
        name="ffn",
    )(x, g, mod, mod, mod, w_up, w_up, w_down, final_norm)


def _to_scan_order(a, n, col_major):
    c = a.shape[-1]
    if col_major:
        rows = n // GRID_W
        return a.reshape(rows, GRID_W, NB, c).transpose(2, 1, 0, 3).reshape(NB, n, c)
    return a.reshape(n, NB, c).transpose(1, 0, 2)


def _from_scan_order(a, n, col_major):
    c = a.shape[-1]
    if col_major:
        rows = n // GRID_W
        return a.reshape(NB, GRID_W, rows, c).transpose(2, 1, 0, 3).reshape(n * NB, c)
    return a.transpose(1, 0, 2).reshape(n * NB, c)


def _pack_w_in(w):
    o_sp = 6 * D
    o_gt = o_sp + D_XBC + 2 * SSM_H
    segs = [w[:, :o_sp], w[:, o_gt:o_gt + 3 * D], w[:, o_sp:o_sp + D_XBC], w[:, o_sp + D_XBC:o_gt]]
    out = jnp.concatenate(segs, axis=1)
    return jnp.pad(out, ((0, 0), (0, D_INP - out.shape[1]))).astype(BF16)


def _pad_to(a, shape):
    return jnp.pad(a, [(0, s - d) for s, d in zip(shape, a.shape)])


def kernel(x, c, ctx, c_ctx, w_mod, b_mod, norm_mix, norm_ffn, w_in, rnn_conv_w, rnn_conv_b, rnn_gate_a_w, rnn_gate_a_b, rnn_gate_x_w, rnn_gate_x_b, rnn_lambda, hy_short_w, hy_short_b, hy_w1, hy_b1, hy_w2, hy_b2, hy_w3, hy_freq, hy_decay, hy_bias, ssm_conv_w, ssm_conv_b, ssm_a_log, ssm_dt_bias, ssm_d, ssm_norm, w_branch, w_out, w_up, w_down, final_norm):
    bsz, n_l, d = x.shape
    n_c = ctx.shape[1]
    depth = w_in.shape[0]
    assert bsz == NB and d == D

    xl = x.transpose(1, 0, 2).reshape(n_l * NB, D)
    sc = ctx.transpose(1, 0, 2).reshape(n_c * NB, D)
    c16 = jnp.concatenate([c, jnp.broadcast_to(c_ctx[None], (NB, D))], axis=0)
    mods = modulation(c16, w_mod, b_mod)

    tabs = {n_l: dft_tables(n_l), n_c: dft_tables(n_c)}
    embs = {n_l: hyena_embedding(n_l), n_c: hyena_embedding(n_c)}
    head_lane = jnp.arange(128)[:, None]
    hp_head = (jnp.arange(D) // SSM_P)[None, :]
    xexp = jnp.stack([(head_lane == hp_head + SSM_H * dd).astype(F32) for dd in range(2)])
    zeros_h = jnp.zeros((NB, D), F32)
    zeros_s = jnp.zeros((NB, SSM_N, D), F32)
    fn = final_norm.reshape(1, D)

    for i in range(depth):
        ctx_out = i < depth - 1
        mod = mods[i]
        w_i = _pack_w_in(w_in[i])
        rp = dict(conv_w=rnn_conv_w[i], conv_b=rnn_conv_b[i].reshape(1, D),
                  wa=rnn_gate_a_w[i].astype(BF16), wx=rnn_gate_x_w[i].astype(BF16),
                  ba=rnn_gate_a_b[i].reshape(2, 1, D), bx=rnn_gate_x_b[i].reshape(2, 1, D),
                  lam=rnn_lambda[i].reshape(2, 1, D))
        hp = dict(short_w=hy_short_w[i], short_b=hy_short_b[i].reshape(1, 3 * D),
                  w1=_pad_to(hy_w1[i], (HY_PAD, HY_PAD)), b1=_pad_to(hy_b1[i].reshape(1, -1), (1, HY_PAD)),
                  w2=_pad_to(hy_w2[i], (HY_PAD, HY_PAD)), b2=_pad_to(hy_b2[i].reshape(1, -1), (1, HY_PAD)),
                  freq=_pad_to(hy_freq[i].reshape(1, -1), (1, HY_PAD)),
                  w3=_pad_to(hy_w3[i], (HY_PAD, 4 * D)), decay=hy_decay[i].reshape(1, 4 * D),
                  bias=hy_bias[i].reshape(2, 1, D))
        a_neg = -jnp.exp(ssm_a_log[i].astype(F32))
        arow = jnp.stack([_pad_to(jnp.concatenate([jnp.zeros((SSM_H * dd,), F32), a_neg[dd]])[None], (1, 128))
                          for dd in range(2)])
        sp = dict(conv_w=ssm_conv_w[i], conv_b=ssm_conv_b[i].reshape(1, D_XBC),
                  dt_bias=_pad_to(ssm_dt_bias[i].reshape(1, 2 * SSM_H), (1, 128)), arow=arow,
                  d_skip=jnp.repeat(ssm_d[i], SSM_P).reshape(1, D), xexp=xexp)
        wb = w_branch[i].astype(BF16)
        wo = w_out[i].astype(BF16)
        wu = w_up[i].astype(BF16)
        wd = w_down[i].astype(BF16)
        g_mix = norm_mix[i].reshape(1, D)
        g_ffn = norm_ffn[i].reshape(1, D)
        nrm_s = ssm_norm[i].reshape(1, D)

        proj_c = in_proj(sc, g_mix, mod, 1, w_i)
        proj_l = in_proj(xl, g_mix, mod, 0, w_i)

        hf_c, hf_last = rglru_pass(proj_c, n_c, 0, False, zeros_h, rp)
        yr_c, hb_last = rglru_pass(proj_c, n_c, 1, True, zeros_h, rp, hf=hf_c)
        hf_l, _ = rglru_pass(proj_l, n_l, 0, False, hf_last, rp)
        yr_l, _ = rglru_pass(proj_l, n_l, 1, True, hb_last, rp, hf=hf_l)

        xbc_c = _to_scan_order(proj_c[:, OFF_XBC:OFF_XBC + D_XBC], n_c, False)
        dt_c = _to_scan_order(proj_c[:, OFF_DT:OFF_DT + 128], n_c, False)
        ys_c, sf, sb = ssd_bidir(xbc_c, dt_c, zeros_s, zeros_s, sp)
        xbc_l = _to_scan_order(proj_l[:, OFF_XBC:OFF_XBC + D_XBC], n_l, True)
        dt_l = _to_scan_order(proj_l[:, OFF_DT:OFF_DT + 128], n_l, True)
        ys_l, _, _ = ssd_bidir(xbc_l, dt_l, sf, sb, sp)
        ys_l = _from_scan_order(ys_l, n_l, True)

        yh_l = hyena_mixer(proj_l, n_l, tabs[n_l], embs[n_l], hp)
        xl = merge(xl, proj_l, yr_l, yh_l, ys_l, nrm_s, wb, wo, mod, 0)
        xl = ffn(xl, g_ffn, mod, 0, wu, wd, fn, final=not ctx_out)

        if ctx_out:
            ys_c = _from_scan_order(ys_c, n_c, False)
            yh_c = hyena_mixer(proj_c, n_c, tabs[n_c], embs[n_c], hp)
            sc = merge(sc, proj_c, yr_c, yh_c, ys_c, nrm_s, wb, wo, mod, 1)
            sc = ffn(sc, g_ffn, mod, 1, wu, wd, fn, final=False)

    return xl.reshape(n_l, NB, D).transpose(1, 0, 2)
```

```python
import functools
import math

import numpy as np
import jax
import jax.numpy as jnp
from jax import lax
from jax.experimental import pallas as pl
from jax.experimental.pallas import tpu as pltpu

F32 = jnp.float32
BF16 = jnp.bfloat16
HI = lax.Precision.HIGHEST

NB = 8
GRID_W = 64
NORM_EPS = 1e-6
LRU_C = 8.0
LRU_BW = 128
SSM_P = 64
SSM_H = 16
SSM_G = 2
SSM_N = 128
SSM_Q = 128
HY_BANDS = 16
HY_PAD = 128

D = 1024
OFF_RX, OFF_RG, OFF_HY, OFF_SZ, OFF_GT, OFF_XBC, OFF_DT = 0, 1024, 2048, 5120, 6144, 9216, 10752
D_XBC = 1536
D_INP = 11264

TM = 512


VMEM_LIMIT = 56 * 1024 * 1024


def _cp(*sem):
    return pltpu.CompilerParams(dimension_semantics=sem, vmem_limit_bytes=VMEM_LIMIT)


def _bdot(a, b):
    return jnp.dot(a, b, preferred_element_type=F32)


def _mod_rows(y, scale8, shift8):
    tm, d = y.shape
    y3 = y.reshape(tm // NB, NB, d)
    if scale8 is not None:
        y3 = y3 * scale8[None]
    if shift8 is not None:
        y3 = y3 + shift8[None]
    return y3.reshape(tm, d)


def _rms(x, g):
    ms = jnp.mean(x * x, axis=-1, keepdims=True)
    return x * lax.rsqrt(ms + NORM_EPS) * g


def _mod_kernel(c_ref, w_ref, b_ref, o_ref):
    c = c_ref[...]
    s = c * jax.nn.sigmoid(c)
    o_ref[0] = jnp.dot(s, w_ref[0], precision=HI, preferred_element_type=F32) + b_ref[0]


def modulation(c16, w_mod, b_mod):
    nl, d, n = w_mod.shape
    tn = 1536
    return pl.pallas_call(
        _mod_kernel,
        grid=(nl, n // tn),
        in_specs=[pl.BlockSpec((16, d), lambda l, j: (0, 0)),
                  pl.BlockSpec((1, d, tn), lambda l, j: (l, 0, j)),
                  pl.BlockSpec((1, 1, tn), lambda l, j: (l, 0, j))],
        out_specs=pl.BlockSpec((1, 16, tn), lambda l, j: (l, 0, j)),
        out_shape=jax.ShapeDtypeStruct((nl, 16, n), F32),
        compiler_params=_cp("parallel", "parallel"),
        name="modulation",
    )(c16, w_mod, b_mod.reshape(nl, 1, n))


def _inproj_kernel(x_ref, g_ref, sh_ref, sc_ref, w_ref, o_ref, xn_ref):
    @pl.when(pl.program_id(1) == 0)
    def _():
        y = _rms(x_ref[...], g_ref[...])
        xn_ref[...] = _mod_rows(y, 1.0 + sc_ref[...], sh_ref[...]).astype(BF16)

    o_ref[...] = _bdot(xn_ref[...], w_ref[...])


def in_proj(x, g, mod, mrow, w):
    m, d = x.shape
    n = w.shape[1]
    tn = 1024
    tm = min(1024, m)
    return pl.pallas_call(
        _inproj_kernel,
        grid=(m // tm, n // tn),
        in_specs=[pl.BlockSpec((tm, d), lambda i, j: (i, 0)),
                  pl.BlockSpec((1, d), lambda i, j: (0, 0)),
                  pl.BlockSpec((NB, d), lambda i, j: (mrow, 0)),
                  pl.BlockSpec((NB, d), lambda i, j: (mrow, 1)),
                  pl.BlockSpec((d, tn), lambda i, j: (0, j))],
        out_specs=pl.BlockSpec((tm, tn), lambda i, j: (i, j)),
        out_shape=jax.ShapeDtypeStruct((m, n), F32),
        scratch_shapes=[pltpu.VMEM((tm, d), BF16)],
        compiler_params=_cp("parallel", "arbitrary"),
        name="in_proj",
    )(x, g, mod, mod, w)


def _expm1_neg(x):
    poly = x * (1.0 + x * (0.5 + x * (1.0 / 6 + x * (1.0 / 24 + x * (1.0 / 120 + x * (1.0 / 720))))))
    return jnp.where(x > -0.1, poly, jnp.exp(x) - 1.0)


def _rglru_kernel(*refs, rev, final, tt, cb, n_t):
    if final:
        (xm, xp, xn, rg, hf, cw, cbias, wa, wx, ba, bx, lam, h0, out, hlast, ext_s, a_s, g_s, h_s) = refs
    else:
        (xm, xp, xn, cw, cbias, wa, wx, ba, bx, lam, h0, out, hlast, ext_s, a_s, g_s, h_s) = refs
    r = tt * NB
    i = pl.program_id(1)
    ti = (n_t - 1 - i) if rev else i

    @pl.when(i == 0)
    def _():
        h_s[...] = h0[...]

    ext_s[0:16, :] = jnp.where(ti == 0, 0.0, xp[...])
    ext_s[16:16 + r, :] = xm[...]
    ext_s[16 + r:24 + r, :] = jnp.where(ti == n_t - 1, 0.0, xn[...])
    lam_v = lam[...]
    sp = jnp.maximum(-lam_v, 0.0) + jnp.log(1.0 + jnp.exp(-jnp.abs(lam_v)))
    sub = 128
    for rc in range(r // sub):
        r0 = rc * sub
        for k in range(cb // LRU_BW):
            ls = slice(k * LRU_BW, (k + 1) * LRU_BW)
            u = cbias[:, ls]
            for j in range(4):
                u = u + cw[j:j + 1, ls] * ext_s[r0 + 8 * j:r0 + 8 * j + sub, ls]
            ub = u.astype(BF16)
            rr = jax.nn.sigmoid(_bdot(ub, wa[k]) + ba[:, ls])
            ii = jax.nn.sigmoid(_bdot(ub, wx[k]) + bx[:, ls])
            la = (-LRU_C) * rr * sp[:, ls]
            a_s[r0:r0 + sub, ls] = jnp.exp(la)
            g_s[r0:r0 + sub, ls] = jnp.sqrt(-_expm1_neg(2.0 * la)) * (ii * u)

    def step(s, h):
        t = (tt - 1 - s) if rev else s
        rows = pl.ds(pl.multiple_of(t * NB, NB), NB)
        h = a_s[rows, :] * h + g_s[rows, :]
        g_s[rows, :] = h
        return h

    h = lax.fori_loop(0, tt, step, h_s[...], unroll=8)
    h_s[...] = h
    hlast[...] = h
    if final:
        for rc in range(r // sub):
            rs = slice(rc * sub, (rc + 1) * sub)
            out[rs, :] = ((hf[rs, :] + g_s[rs, :]) * jax.nn.gelu(rg[rs, :])).astype(out.dtype)
    else:
        out[...] = g_s[...]


def rglru_pass(proj, n, d_idx, rev, h0, p, hf=None):
    final = hf is not None
    tt = min(64, n)
    cb = D
    r = tt * NB
    n_t = n // tt
    nc = D // cb

    def tix(i):
        return (n_t - 1 - i) if rev else i

    main = pl.BlockSpec((r, cb), lambda c, i: (tix(i), c))
    prev = pl.BlockSpec((16, cb), lambda c, i: (jnp.maximum(tix(i) * (r // 16) - 1, 0), c))
    nxt = pl.BlockSpec((NB, cb), lambda c, i: (jnp.minimum((tix(i) + 1) * tt, n - 1), c))
    vec = pl.BlockSpec((1, cb), lambda c, i: (0, c))
    wsp = pl.BlockSpec((cb // LRU_BW, LRU_BW, LRU_BW), lambda c, i: (c, 0, 0))
    in_specs = [main, prev, nxt]
    args = [proj, proj, proj]
    if final:
        in_specs += [pl.BlockSpec((r, cb), lambda c, i: (tix(i), OFF_RG // cb + c)), main]
        args += [proj, hf]
    in_specs += [pl.BlockSpec((4, cb), lambda c, i: (0, c)), vec, wsp, wsp, vec, vec, vec,
                 pl.BlockSpec((NB, cb), lambda c, i: (0, c))]
    args += [p["conv_w"], p["conv_b"], p["wa"][d_idx], p["wx"][d_idx], p["ba"][d_idx], p["bx"][d_idx],
             p["lam"][d_idx], h0]
    out, hlast = pl.pallas_call(
        functools.partial(_rglru_kernel, rev=rev, final=final, tt=tt, cb=cb, n_t=n_t),
        grid=(nc, n_t),
        in_specs=in_specs,
        out_specs=[main, pl.BlockSpec((NB, cb), lambda c, i: (0, c))],
        out_shape=[jax.ShapeDtypeStruct((n * NB, D), BF16 if final else F32),
                   jax.ShapeDtypeStruct((NB, D), F32)],
        scratch_shapes=[pltpu.VMEM((r + 24, cb), F32), pltpu.VMEM((r, cb), F32), pltpu.VMEM((r, cb), F32),
                        pltpu.VMEM((NB, cb), F32)],
        compiler_params=_cp("parallel", "arbitrary"),
        name="rglru_bwd" if rev else "rglru_fwd",
    )(*args)
    return out, hlast


def _hyshort_kernel(xm, xp, xn, w, b, o, ob, *, n_t):
    i = pl.program_id(0)
    r = xm.shape[0]
    prev = jnp.where(i == 0, 0.0, xp[...])
    nxt = jnp.where(i == n_t - 1, 0.0, xn[...])
    x = xm[...]
    sh_prev = jnp.concatenate([prev, x[:r - NB]], axis=0)
    sh_next = jnp.concatenate([x[NB:], nxt], axis=0)
    u = w[0:1, :] * sh_prev + w[1:2, :] * x + w[2:3, :] * sh_next + b[...]
    o[...] = u
    ob[...] = u.astype(ob.dtype)


def hyena_short(proj, n, w, b):
    tt = min(64, n)
    r = tt * NB
    n_t = n // tt
    cb = 1024
    c0 = OFF_HY // cb
    out = pl.BlockSpec((r, cb), lambda i, c: (i, c))
    return pl.pallas_call(
        functools.partial(_hyshort_kernel, n_t=n_t),
        grid=(n_t, 3 * D // cb),
        in_specs=[pl.BlockSpec((r, cb), lambda i, c: (i, c0 + c)),
                  pl.BlockSpec((NB, cb), lambda i, c: (jnp.maximum(i * tt - 1, 0), c0 + c)),
                  pl.BlockSpec((NB, cb), lambda i, c: (jnp.minimum((i + 1) * tt, n - 1), c0 + c)),
                  pl.BlockSpec((3, cb), lambda i, c: (0, c)),
                  pl.BlockSpec((1, cb), lambda i, c: (0, c))],
        out_specs=[out, out],
        out_shape=[jax.ShapeDtypeStruct((n * NB, 3 * D), F32), jax.ShapeDtypeStruct((n * NB, 3 * D), BF16)],
        compiler_params=_cp("parallel", "parallel"),
        name="hyena_short",
    )(proj, proj, proj, w, b)


def _hyfilt_kernel(emb, w1, b1, w2, b2, fr, w3f, w3b, dcf, dcb, of, ob):
    n = emb.shape[0]
    f = fr[...]
    h = jnp.sin(f * (jnp.dot(emb[...], w1[...], precision=HI, preferred_element_type=F32) + b1[...]))
    h = jnp.sin(f * (jnp.dot(h, w2[...], precision=HI, preferred_element_type=F32) + b2[...]))
    row = lax.broadcasted_iota(jnp.int32, (n, 1), 0)
    t = row.astype(F32) * (1.0 / (n - 1))
    kf = jnp.dot(h, w3f[...], precision=HI, preferred_element_type=F32) * jnp.exp(-t * dcf[...])
    kb = jnp.dot(h, w3b[...], precision=HI, preferred_element_type=F32) * jnp.exp(-t * dcb[...])
    kb = jnp.where(row == 0, 0.0, kb)
    nrm = jnp.sum(jnp.abs(kf), axis=0, keepdims=True) + jnp.sum(jnp.abs(kb), axis=0, keepdims=True)
    inv = 1.0 / nrm
    of[...] = (kf * inv).astype(of.dtype)
    ob[...] = (kb * inv).astype(ob.dtype)


def hyena_filters(emb, p):
    n = emb.shape[0]
    tn = 256
    nj = 2 * D // tn
    full = lambda s: pl.BlockSpec(s, lambda j: (0, 0))
    return pl.pallas_call(
        _hyfilt_kernel,
        grid=(nj,),
        in_specs=[full((n, HY_PAD)), full((HY_PAD, HY_PAD)), full((1, HY_PAD)), full((HY_PAD, HY_PAD)),
                  full((1, HY_PAD)), full((1, HY_PAD)),
                  pl.BlockSpec((HY_PAD, tn), lambda j: (0, j)), pl.BlockSpec((HY_PAD, tn), lambda j: (0, nj + j)),
                  pl.BlockSpec((1, tn), lambda j: (0, j)), pl.BlockSpec((1, tn), lambda j: (0, nj + j))],
        out_specs=[pl.BlockSpec((n, tn), lambda j: (0, j)), pl.BlockSpec((n, tn), lambda j: (0, j))],
        out_shape=[jax.ShapeDtypeStruct((n, 2 * D), BF16), jax.ShapeDtypeStruct((n, 2 * D), BF16)],
        compiler_params=_cp("parallel"),
        name="hyena_filters",
    )(emb, p["w1"], p["b1"], p["w2"], p["b2"], p["freq"], p["w3"], p["w3"], p["decay"], p["decay"])


def _dftp_kernel(c_ref, s_ref, x_ref, a_ref, b_ref):
    x = x_ref[...]
    a_ref[...] = _bdot(c_ref[...], x)
    b_ref[...] = _bdot(s_ref[...], x)


def _kcombine_kernel(af_ref, ab_ref, bf_ref, bb_ref, kre_ref, kim_ref):
    i = pl.program_id(0)
    kre_ref[...] = af_ref[...] + ab_ref[...]
    bf, bb = bf_ref[...], bb_ref[...]
    first = (lax.broadcasted_iota(jnp.int32, bf.shape, 0) == 0) & (i == 0)
    kim_ref[...] = jnp.where(first, bf + bb, bb - bf)


def hyena_kspec(tabs, kf, kb):
    cmat, smat, _ = tabs
    n = cmat.shape[0]
    tf = min(1024, n)
    tn = 512
    x = jnp.concatenate([kf, kb], axis=1)
    nj = x.shape[1] // D
    out = pl.BlockSpec((tf, tn), lambda i, j: (i, j))
    a, b = pl.pallas_call(
        _dftp_kernel,
        grid=(n // tf, x.shape[1] // tn),
        in_specs=[_table_spec(tf, n), _table_spec(tf, n), pl.BlockSpec((n, tn), lambda i, j: (0, j))],
        out_specs=[out, out],
        out_shape=[jax.ShapeDtypeStruct(x.shape, F32), jax.ShapeDtypeStruct(x.shape, F32)],
        compiler_params=_cp("parallel", "arbitrary"),
        name="hyena_kdft",
    )(cmat, smat, x)
    fwd = pl.BlockSpec((tf, D), lambda i, j: (i, j))
    bwd = pl.BlockSpec((tf, D), lambda i, j: (i, nj // 2 + j))
    return pl.pallas_call(
        _kcombine_kernel,
        grid=(n // tf, nj // 2),
        in_specs=[fwd, bwd, fwd, bwd],
        out_specs=[fwd, fwd],
        out_shape=[jax.ShapeDtypeStruct((n, 2 * D), F32), jax.ShapeDtypeStruct((n, 2 * D), F32)],
        compiler_params=_cp("parallel", "parallel"),
        name="hyena_kcombine",
    )(a, a, b, b)


def _dftf_kernel(c_ref, s_ref, x_ref, kre_ref, kim_ref, qa_ref, qb_ref, *, inv_n):
    i = pl.program_id(0)
    x = x_ref[...]
    a = _bdot(c_ref[...], x)
    b = _bdot(s_ref[...], x)
    kre, kim = kre_ref[...], kim_ref[...]
    first = (lax.broadcasted_iota(jnp.int32, a.shape, 0) == 0) & (i == 0)
    qa = jnp.where(first, inv_n * (a * kre), (2.0 * inv_n) * (a * kre + b * kim))
    qb = jnp.where(first, inv_n * (b * kim), (2.0 * inv_n) * (b * kre - a * kim))
    qa_ref[...] = qa.astype(qa_ref.dtype)
    qb_ref[...] = qb.astype(qb_ref.dtype)


def _table_spec(tf, n):
    return pl.BlockSpec((tf, n), lambda i, j: (i, 0), pipeline_mode=pl.Buffered(1))


def hyena_fwd(tabs, x2d, part, nparts, kre, kim, order):
    cmat, smat, _ = tabs
    n = cmat.shape[0]
    tf = min(1024, n)
    tn = 512
    nb = D // tn
    out = pl.BlockSpec((tf, tn), lambda i, j: (i, j))
    kspec = pl.BlockSpec((tf, tn), lambda i, j: (i, order * nb + j % nb))
    return pl.pallas_call(
        functools.partial(_dftf_kernel, inv_n=1.0 / (2 * n)),
        grid=(n // tf, NB * nb),
        in_specs=[_table_spec(tf, n), _table_spec(tf, n),
                  pl.BlockSpec((n, tn), lambda i, j: (0, ((j // nb) * nparts + part) * nb + j % nb)),
                  kspec, kspec],
        out_specs=[out, out],
        out_shape=[jax.ShapeDtypeStruct((n, NB * D), BF16), jax.ShapeDtypeStruct((n, NB * D), BF16)],
        compiler_params=_cp("parallel", "arbitrary"),
        name="hyena_fwd",
    )(cmat, smat, x2d, kre, kim)


def _dfti_kernel(c_ref, st_ref, qa_ref, qb_ref, xk_ref, z_ref, bias_ref, *o_refs):
    y = _bdot(c_ref[...], qa_ref[...]) + _bdot(st_ref[...], qb_ref[...])
    o = xk_ref[...] * (y + bias_ref[0] * z_ref[...])
    for o_ref in o_refs:
        o_ref[...] = o.astype(o_ref.dtype)


def hyena_inv(tabs, qa, qb, u2d, xpart, z2d, zpart, znparts, bias, order, out_dtypes):
    cmat, _, smat_t = tabs
    n = cmat.shape[0]
    tf = min(1024, n)
    tn = 256
    nb = D // tn
    out = pl.BlockSpec((tf, tn), lambda i, j: (i, j))
    q = pl.BlockSpec((n, tn), lambda i, j: (0, j))
    return pl.pallas_call(
        _dfti_kernel,
        grid=(n // tf, NB * nb),
        in_specs=[_table_spec(tf, n), _table_spec(tf, n), q, q,
                  pl.BlockSpec((tf, tn), lambda i, j: (i, ((j // nb) * 3 + xpart) * nb + j % nb)),
                  pl.BlockSpec((tf, tn), lambda i, j: (i, ((j // nb) * znparts + zpart) * nb + j % nb)),
                  pl.BlockSpec((1, 1, tn), lambda i, j: (order, 0, j % nb))],
        out_specs=[out] * len(out_dtypes),
        out_shape=[jax.ShapeDtypeStruct((n, NB * D), dt) for dt in out_dtypes],
        compiler_params=_cp("parallel", "arbitrary"),
        name="hyena_inv",
    )(cmat, smat_t, qa, qb, u2d, z2d, bias)


DFT_R = 64


def _dft_table_kernel(ac_ref, as_ref, bc_ref, bs_ref, c_ref, s_ref, st_ref):
    f1 = pl.program_id(0)
    ac, as_ = ac_ref[0], as_ref[0]
    bc, bs = bc_ref[...], bs_ref[...]
    c = ac * bc - as_ * bs
    s = as_ * bc + ac * bs
    row = lax.broadcasted_iota(jnp.int32, c.shape, 0)
    col = lax.broadcasted_iota(jnp.int32, c.shape, 1)
    alt_col = jnp.where(col % 2 == 0, 1.0, -1.0)
    alt_row = jnp.where(row % 2 == 0, 1.0, -1.0)
    c_ref[...] = c.astype(c_ref.dtype)
    s_ref[...] = jnp.where((row == 0) & (f1 == 0), alt_col, s).astype(s_ref.dtype)
    st_ref[...] = jnp.where(col == 0, alt_row, s).astype(st_ref.dtype)


def dft_tables(n):
    r = min(DFT_R, n)
    t = np.arange(n, dtype=np.int64)[None, :]
    f1 = np.arange(n // r, dtype=np.int64)[:, None]
    f0 = np.arange(r, dtype=np.int64)[:, None]
    ang_a = ((r * f1 * t) % (2 * n)).astype(np.float64) * (np.pi / n)
    ang_b = ((f0 * t) % (2 * n)).astype(np.float64) * (np.pi / n)
    ac = jnp.asarray(np.cos(ang_a), F32).reshape(n // r, 1, n)
    as_ = jnp.asarray(np.sin(ang_a), F32).reshape(n // r, 1, n)
    bc = jnp.asarray(np.cos(ang_b), F32)
    bs = jnp.asarray(np.sin(ang_b), F32)
    arow = pl.BlockSpec((1, 1, n), lambda i: (i, 0, 0))
    bfull = pl.BlockSpec((r, n), lambda i: (0, 0))
    out = pl.BlockSpec((r, n), lambda i: (i, 0))
    return tuple(pl.pallas_call(
        _dft_table_kernel,
        grid=(n // r,),
        in_specs=[arow, arow, bfull, bfull],
        out_specs=[out, out, out],
        out_shape=[jax.ShapeDtypeStruct((n, n), BF16)] * 3,
        compiler_params=_cp("parallel"),
        name="dft_tables",
    )(ac, as_, bc, bs))


def hyena_embedding(n):
    t = np.linspace(0.0, 1.0, n)[:, None]
    bands = np.linspace(1e-4, HY_BANDS - 1, HY_BANDS)
    ang = (2.0 * math.pi / n) * np.arange(n, dtype=np.float64)[:, None] * bands[None]
    emb = np.concatenate([t, np.cos(ang), np.sin(ang)], axis=-1)
    return jnp.asarray(np.pad(emb, ((0, 0), (0, HY_PAD - emb.shape[1]))), F32)


def hyena_mixer(proj, n, tabs, emb, p):
    u, ub = hyena_short(proj, n, p["short_w"], p["short_b"])
    kf, kb = hyena_filters(emb, p)
    kre, kim = hyena_kspec(tabs, kf, kb)
    u2d = u.reshape(n, NB * 3 * D)
    qa, qb = hyena_fwd(tabs, ub.reshape(n, NB * 3 * D), 0, 3, kre, kim, 0)
    z2, z2b = hyena_inv(tabs, qa, qb, u2d, 1, u2d, 0, 3, p["bias"], 0, (F32, BF16))
    qa, qb = hyena_fwd(tabs, z2b, 0, 1, kre, kim, 1)
    (y,) = hyena_inv(tabs, qa, qb, u2d, 2, z2, 0, 1, p["bias"], 1, (BF16,))
    return y.reshape(n * NB, D)


def _softplus(x):
    return jnp.maximum(x, 0.0) + jnp.log(1.0 + jnp.exp(-jnp.abs(x)))


def _ssd_kernel(*refs, rev, nc):
    if rev:
        (xm, xp, xn, dt_ref, yprev, cw, cbias, dtb, arow, xexp, h0, y_ref, hlast, ext_s, s_s) = refs
    else:
        (xm, xp, xn, dt_ref, cw, cbias, dtb, arow, dsk, xexp, h0, y_ref, hlast, ext_s, s_s) = refs
    q = SSM_Q
    i = pl.program_id(1)
    ci = (nc - 1 - i) if rev else i

    @pl.when(i == 0)
    def _():
        s_s[...] = h0[0]

    ext_s[0:8, :] = jnp.where(ci == 0, 0.0, xp[0])
    ext_s[8:8 + q, :] = xm[0]
    ext_s[8 + q:16 + q, :] = jnp.where(ci == nc - 1, 0.0, xn[0])
    acc = cbias[...]
    for j in range(4):
        acc = acc + cw[j:j + 1, :] * ext_s[pl.ds(6 + j, q), :]
    xbc = acc * jax.nn.sigmoid(acc)
    xs = xbc[:, :D]
    dtp = _softplus(dt_ref[0] + dtb[...])
    dta = dtp * arow[...]
    row = lax.broadcasted_iota(jnp.int32, (q, q), 0)
    col = lax.broadcasted_iota(jnp.int32, (q, q), 1)
    mask = (row <= col) if rev else (row >= col)
    cum = jnp.dot(mask.astype(F32), dta, precision=HI, preferred_element_type=F32)
    cum_t = cum.T
    cum_e = jnp.dot(cum, xexp[...], precision=HI, preferred_element_type=F32)
    dt_e = jnp.dot(dtp, xexp[...], precision=HI, preferred_element_type=F32)
    last = 0 if rev else q - 1
    cum_last = cum_e[last:last + 1, :]
    xdt = xs * dt_e
    xdt_b = xdt.astype(BF16)
    s_old = s_s[...]
    s_old_b = s_old.astype(BF16)
    w_b = (jnp.exp(cum_last - cum_e) * xdt).astype(BF16)
    lane = lax.broadcasted_iota(jnp.int32, (q, 2 * SSM_P), 1)
    gw = D // SSM_G
    e_per_g = SSM_H // SSM_G
    slot0 = SSM_H if rev else 0
    y_cols = []
    s_cols = []
    for g in range(SSM_G):
        bg = xbc[:, D + g * SSM_N:D + (g + 1) * SSM_N].astype(BF16)
        cg = xbc[:, D + SSM_G * SSM_N + g * SSM_N:D + SSM_G * SSM_N + (g + 1) * SSM_N].astype(BF16)
        cbm = lax.dot_general(cg, bg, (((1,), (1,)), ((), ())), preferred_element_type=F32)
        for pr in range(e_per_g // 2):
            c0 = g * gw + pr * 2 * SSM_P
            xpair = xdt[:, c0:c0 + 2 * SSM_P]
            yp = jnp.zeros((q, 2 * SSM_P), F32)
            for hh in range(2):
                slot = slot0 + g * e_per_g + pr * 2 + hh
                seg = cum[:, slot:slot + 1] - cum_t[slot:slot + 1, :]
                dec = jnp.exp(jnp.where(mask, seg, -jnp.inf))
                mh = (cbm * dec).astype(BF16)
                in_head = (lane >= hh * SSM_P) & (lane < (hh + 1) * SSM_P)
                yp = yp + _bdot(mh, jnp.where(in_head, xpair, 0.0).astype(BF16))
            y_cols.append(yp)
        gs = slice(g * gw, (g + 1) * gw)
        y_state = _bdot(cg, s_old_b[:, gs])
        y_cols[-(e_per_g // 2):] = [
            y_cols[-(e_per_g // 2) + pr] + jnp.exp(cum_e[:, g * gw + pr * 2 * SSM_P:g * gw + (pr + 1) * 2 * SSM_P])
            * y_state[:, pr * 2 * SSM_P:(pr + 1) * 2 * SSM_P] for pr in range(e_per_g // 2)]
        s_new = jnp.exp(cum_last[:, gs]) * s_old[:, gs] + lax.dot_general(
            bg, w_b[:, gs], (((0,), (0,)), ((), ())), preferred_element_type=F32)
        s_cols.append(s_new)
    y = jnp.concatenate(y_cols, axis=1)
    s_new = jnp.concatenate(s_cols, axis=1)
    s_s[...] = s_new
    hlast[0] = s_new
    if rev:
        y_ref[0] = y + yprev[0]
    else:
        y_ref[0] = y + dsk[...] * xs


def ssd_pass(xbc, dt, rev, h0, p, yprev=None):
    _, n, _ = xbc.shape
    q = SSM_Q
    nc = n // q
    d_idx = 1 if rev else 0

    def cix(i):
        return (nc - 1 - i) if rev else i

    in_specs = [pl.BlockSpec((1, q, D_XBC), lambda b, i: (b, cix(i), 0)),
                pl.BlockSpec((1, NB, D_XBC), lambda b, i: (b, jnp.maximum(cix(i) * (q // NB) - 1, 0), 0)),
                pl.BlockSpec((1, NB, D_XBC), lambda b, i: (b, jnp.minimum((cix(i) + 1) * (q // NB), n // NB - 1), 0)),
                pl.BlockSpec((1, q, 128), lambda b, i: (b, cix(i), 0))]
    args = [xbc, xbc, xbc, dt]
    ymain = pl.BlockSpec((1, q, D), lambda b, i: (b, cix(i), 0))
    if rev:
        in_specs.append(ymain)
        args.append(yprev)
    full = lambda s: pl.BlockSpec(s, lambda b, i: (0,) * len(s))
    in_specs += [full((4, D_XBC)), full((1, D_XBC)), full((1, 128)), full((1, 128))]
    args += [p["conv_w"], p["conv_b"], p["dt_bias"], p["arow"][d_idx]]
    if not rev:
        in_specs.append(full((1, D)))
        args.append(p["d_skip"])
    in_specs += [full((128, D)), pl.BlockSpec((1, SSM_N, D), lambda b, i: (b, 0, 0))]
    args += [p["xexp"][d_idx], h0]
    y, hlast = pl.pallas_call(
        functools.partial(_ssd_kernel, rev=rev, nc=nc),
        grid=(NB, nc),
        in_specs=in_specs,
        out_specs=[ymain, pl.BlockSpec((1, SSM_N, D), lambda b, i: (b, 0, 0))],
        out_shape=[jax.ShapeDtypeStruct((NB, n, D), F32), jax.ShapeDtypeStruct((NB, SSM_N, D), F32)],
        scratch_shapes=[pltpu.VMEM((q + 16, D_XBC), F32), pltpu.VMEM((SSM_N, D), F32)],
        compiler_params=_cp("parallel", "arbitrary"),
        name="ssd_bwd" if rev else "ssd_fwd",
    )(*args)
    return y, hlast


def ssd_bidir(xbc, dt, hf0, hb0, p):
    yf, sf = ssd_pass(xbc, dt, False, hf0, p)
    y, sb = ssd_pass(xbc, dt, True, hb0, p, yprev=yf)
    return y, sf, sb


def _merge_kernel(x_ref, gt_ref, yr_ref, yh_ref, ys_ref, z_ref, nw_ref, wb_ref, wo_ref, g1_ref, o_ref):
    z = z_ref[...]
    ys = _rms(ys_ref[...] * (z * jax.nn.sigmoid(z)), nw_ref[...])
    m = jax.nn.sigmoid(gt_ref[:, 0:D]) * _bdot(yr_ref[...], wb_ref[0])
    m = m + jax.nn.sigmoid(gt_ref[:, D:2 * D]) * _bdot(yh_ref[...], wb_ref[1])
    m = m + jax.nn.sigmoid(gt_ref[:, 2 * D:3 * D]) * _bdot(ys.astype(BF16), wb_ref[2])
    o = _bdot(m.astype(BF16), wo_ref[...])
    o_ref[...] = x_ref[...] + _mod_rows(o, g1_ref[...], None)


def merge(x, proj, yr, yh, ys, norm_w, wb, wo, mod, mrow):
    m = x.shape[0]
    row = lambda w, c: pl.BlockSpec((TM, w), lambda i: (i, c))
    return pl.pallas_call(
        _merge_kernel,
        grid=(m // TM,),
        in_specs=[row(D, 0), row(3 * D, OFF_GT // (3 * D)), row(D, 0), row(D, 0), row(D, 0), row(D, OFF_SZ // D),
                  pl.BlockSpec((1, D), lambda i: (0, 0)),
                  pl.BlockSpec((3, D, D), lambda i: (0, 0, 0)),
                  pl.BlockSpec((D, D), lambda i: (0, 0)),
                  pl.BlockSpec((NB, D), lambda i: (mrow, 2))],
        out_specs=row(D, 0),
        out_shape=jax.ShapeDtypeStruct((m, D), F32),
        compiler_params=_cp("parallel"),
        name="merge",
    )(x, proj, yr, yh, ys, proj, norm_w, wb, wo, mod)


def _ffn_kernel(x_ref, g_ref, sh_ref, sc_ref, g2_ref, wg_ref, wu_ref, wd_ref, fn_ref, o_ref, xn_s, acc_s, *, nf, final):
    f = pl.program_id(1)

    @pl.when(f == 0)
    def _():
        y = _rms(x_ref[...], g_ref[...])
        xn_s[...] = _mod_rows(y, 1.0 + sc_ref[...], sh_ref[...]).astype(BF16)
        acc_s[...] = jnp.zeros_like(acc_s)

    xn = xn_s[...]
    hg = _bdot(xn, wg_ref[...])
    hu = _bdot(xn, wu_ref[...])
    hh = (hg * jax.nn.sigmoid(hg) * hu).astype(BF16)
    acc_s[...] += _bdot(hh, wd_ref[...])

    @pl.when(f == nf - 1)
    def _():
        o = x_ref[...] + _mod_rows(acc_s[...], g2_ref[...], None)
        if final:
            o = _rms(o, fn_ref[...])
        o_ref[...] = o


def ffn(x, g, mod, mrow, w_up, w_down, final_norm, final):
    m = x.shape[0]
    d_ff = w_down.shape[0]
    nf = 2
    tf = d_ff // nf
    vec = pl.BlockSpec((1, D), lambda i, f: (0, 0))
    return pl.pallas_call(
        functools.partial(_ffn_kernel, nf=nf, final=final),
        grid=(m // TM, nf),
        in_specs=[pl.BlockSpec((TM, D), lambda i, f: (i, 0)), vec,
                  pl.BlockSpec((NB, D), lambda i, f: (mrow, 3)),
                  pl.BlockSpec((NB, D), lambda i, f: (mrow, 4)),
                  pl.BlockSpec((NB, D), lambda i, f: (mrow, 5)),
                  pl.BlockSpec((D, tf), lambda i, f: (0, f)),
                  pl.BlockSpec((D, tf), lambda i, f: (0, nf + f)),
                  pl.BlockSpec((tf, D), lambda i, f: (f, 0)), vec],
        out_specs=pl.BlockSpec((TM, D), lambda i, f: (i, 0)),
        out_shape=jax.ShapeDtypeStruct((m, D), F32),
        scratch_shapes=[pltpu.VMEM((TM, D), BF16), pltpu.VMEM((TM, D), F32)],
        compiler_params=_cp("parallel", "arbitrary"),
        name="ffn",
    )(x, g, mod, mod, mod, w_up, w_up, w_down, final_norm)


def _to_scan_order(a, n, col_major):
    c = a.shape[-1]
    if col_major:
        rows = n // GRID_W
        return a.reshape(rows, GRID_W, NB, c).transpose(2, 1, 0, 3).reshape(NB, n, c)
    return a.reshape(n, NB, c).transpose(1, 0, 2)


def _from_scan_order(a, n, col_major):
    c = a.shape[-1]
    if col_major:
        rows = n // GRID_W
        return a.reshape(NB, GRID_W, rows, c).transpose(2, 1, 0, 3).reshape(n * NB, c)
    return a.transpose(1, 0, 2).reshape(n * NB, c)


def _pack_w_in(w):
    o_sp = 6 * D
    o_gt = o_sp + D_XBC + 2 * SSM_H
    segs = [w[:, :o_sp], w[:, o_gt:o_gt + 3 * D], w[:, o_sp:o_sp + D_XBC], w[:, o_sp + D_XBC:o_gt]]
    out = jnp.concatenate(segs, axis=1)
    return jnp.pad(out, ((0, 0), (0, D_INP - out.shape[1]))).astype(BF16)


def _pad_to(a, shape):
    return jnp.pad(a, [(0, s - d) for s, d in zip(shape, a.shape)])


def kernel(x, c, ctx, c_ctx, w_mod, b_mod, norm_mix, norm_ffn, w_in, rnn_conv_w, rnn_conv_b, rnn_gate_a_w, rnn_gate_a_b, rnn_gate_x_w, rnn_gate_x_b, rnn_lambda, hy_short_w, hy_short_b, hy_w1, hy_b1, hy_w2, hy_b2, hy_w3, hy_freq, hy_decay, hy_bias, ssm_conv_w, ssm_conv_b, ssm_a_log, ssm_dt_bias, ssm_d, ssm_norm, w_branch, w_out, w_up, w_down, final_norm):
    bsz, n_l, d = x.shape
    n_c = ctx.shape[1]
    depth = w_in.shape[0]
    assert bsz == NB and d == D

    xl = x.transpose(1, 0, 2).reshape(n_l * NB, D)
    sc = ctx.transpose(1, 0, 2).reshape(n_c * NB, D)
    c16 = jnp.concatenate([c, jnp.broadcast_to(c_ctx[None], (NB, D))], axis=0)
    mods = modulation(c16, w_mod, b_mod)

    tabs = {n_l: dft_tables(n_l), n_c: dft_tables(n_c)}
    embs = {n_l: hyena_embedding(n_l), n_c: hyena_embedding(n_c)}
    head_lane = jnp.arange(128)[:, None]
    hp_head = (jnp.arange(D) // SSM_P)[None, :]
    xexp = jnp.stack([(head_lane == hp_head + SSM_H * dd).astype(F32) for dd in range(2)])
    zeros_h = jnp.zeros((NB, D), F32)
    zeros_s = jnp.zeros((NB, SSM_N, D), F32)
    fn = final_norm.reshape(1, D)

    for i in range(depth):
        ctx_out = i < depth - 1
        mod = mods[i]
        w_i = _pack_w_in(w_in[i])
        rp = dict(conv_w=rnn_conv_w[i], conv_b=rnn_conv_b[i].reshape(1, D),
                  wa=rnn_gate_a_w[i].astype(BF16), wx=rnn_gate_x_w[i].astype(BF16),
                  ba=rnn_gate_a_b[i].reshape(2, 1, D), bx=rnn_gate_x_b[i].reshape(2, 1, D),
                  lam=rnn_lambda[i].reshape(2, 1, D))
        hp = dict(short_w=hy_short_w[i], short_b=hy_short_b[i].reshape(1, 3 * D),
                  w1=_pad_to(hy_w1[i], (HY_PAD, HY_PAD)), b1=_pad_to(hy_b1[i].reshape(1, -1), (1, HY_PAD)),
                  w2=_pad_to(hy_w2[i], (HY_PAD, HY_PAD)), b2=_pad_to(hy_b2[i].reshape(1, -1), (1, HY_PAD)),
                  freq=_pad_to(hy_freq[i].reshape(1, -1), (1, HY_PAD)),
                  w3=_pad_to(hy_w3[i], (HY_PAD, 4 * D)), decay=hy_decay[i].reshape(1, 4 * D),
                  bias=hy_bias[i].reshape(2, 1, D))
        a_neg = -jnp.exp(ssm_a_log[i].astype(F32))
        arow = jnp.stack([_pad_to(jnp.concatenate([jnp.zeros((SSM_H * dd,), F32), a_neg[dd]])[None], (1, 128))
                          for dd in range(2)])
        sp = dict(conv_w=ssm_conv_w[i], conv_b=ssm_conv_b[i].reshape(1, D_XBC),
                  dt_bias=_pad_to(ssm_dt_bias[i].reshape(1, 2 * SSM_H), (1, 128)), arow=arow,
                  d_skip=jnp.repeat(ssm_d[i], SSM_P).reshape(1, D), xexp=xexp)
        wb = w_branch[i].astype(BF16)
        wo = w_out[i].astype(BF16)
        wu = w_up[i].astype(BF16)
        wd = w_down[i].astype(BF16)
        g_mix = norm_mix[i].reshape(1, D)
        g_ffn = norm_ffn[i].reshape(1, D)
        nrm_s = ssm_norm[i].reshape(1, D)

        proj_c = in_proj(sc, g_mix, mod, 1, w_i)
        proj_l = in_proj(xl, g_mix, mod, 0, w_i)

        hf_c, hf_last = rglru_pass(proj_c, n_c, 0, False, zeros_h, rp)
        yr_c, hb_last = rglru_pass(proj_c, n_c, 1, True, zeros_h, rp, hf=hf_c)
        hf_l, _ = rglru_pass(proj_l, n_l, 0, False, hf_last, rp)
        yr_l, _ = rglru_pass(proj_l, n_l, 1, True, hb_last, rp, hf=hf_l)

        xbc_c = _to_scan_order(proj_c[:, OFF_XBC:OFF_XBC + D_XBC], n_c, False)
        dt_c = _to_scan_order(proj_c[:, OFF_DT:OFF_DT + 128], n_c, False)
        ys_c, sf, sb = ssd_bidir(xbc_c, dt_c, zeros_s, zeros_s, sp)
        xbc_l = _to_scan_order(proj_l[:, OFF_XBC:OFF_XBC + D_XBC], n_l, True)
        dt_l = _to_scan_order(proj_l[:, OFF_DT:OFF_DT + 128], n_l, True)
        ys_l, _, _ = ssd_bidir(xbc_l, dt_l, sf, sb, sp)
        ys_l = _from_scan_order(ys_l, n_l, True)

        yh_l = hyena_mixer(proj_l, n_l, tabs[n_l], embs[n_l], hp)
        xl = merge(xl, proj_l, yr_l, yh_l, ys_l, nrm_s, wb, wo, mod, 0)
        xl = ffn(xl, g_ffn, mod, 0, wu, wd, fn, final=not ctx_out)

        if ctx_out:
            ys_c = _from_scan_order(ys_c, n_c, False)
            yh_c = hyena_mixer(proj_c, n_c, tabs[n_c], embs[n_c], hp)
            sc = merge(sc, proj_c, yr_c, yh_c, ys_c, nrm_s, wb, wo, mod, 1)
            sc = ffn(sc, g_ffn, mod, 1, wu, wd, fn, final=False)

    return xl.reshape(n_l, NB, D).transpose(1, 0, 2)
```

```python
import functools
import math

import numpy as np
import jax
import jax.numpy as jnp
from jax import lax
from jax.experimental import pallas as pl
from jax.experimental.pallas import tpu as pltpu

F32 = jnp.float32
BF16 = jnp.bfloat16
HI = lax.Precision.HIGHEST

NB = 8
GRID_W = 64
NORM_EPS = 1e-6
LRU_C = 8.0
LRU_BW = 128
SSM_P = 64
SSM_H = 16
SSM_G = 2
SSM_N = 128
SSM_Q = 128
HY_BANDS = 16
HY_PAD = 128

D = 1024
OFF_RX, OFF_RG, OFF_HY, OFF_SZ, OFF_XBC, OFF_DT, OFF_GT = 0, 1024, 2048, 5120, 6144, 7680, 8192
D_SSD = 2048
D_XBC = 1536
D_INP = 11264

TM = 512


VMEM_LIMIT = 56 * 1024 * 1024


def _cp(*sem):
    return pltpu.CompilerParams(dimension_semantics=sem, vmem_limit_bytes=VMEM_LIMIT)


def _bdot(a, b):
    return jnp.dot(a, b, preferred_element_type=F32)


def _mod_rows(y, scale8, shift8):
    tm, d = y.shape
    y3 = y.reshape(tm // NB, NB, d)
    if scale8 is not None:
        y3 = y3 * scale8[None]
    if shift8 is not None:
        y3 = y3 + shift8[None]
    return y3.reshape(tm, d)


def _rms(x, g):
    ms = jnp.mean(x * x, axis=-1, keepdims=True)
    return x * lax.rsqrt(ms + NORM_EPS) * g


def _mod_kernel(c_ref, w_ref, b_ref, o_ref):
    c = c_ref[...]
    s = c * jax.nn.sigmoid(c)
    o_ref[0] = jnp.dot(s, w_ref[0], precision=HI, preferred_element_type=F32) + b_ref[0]


def modulation(c16, w_mod, b_mod):
    nl, d, n = w_mod.shape
    tn = 1536
    return pl.pallas_call(
        _mod_kernel,
        grid=(nl, n // tn),
        in_specs=[pl.BlockSpec((16, d), lambda l, j: (0, 0)),
                  pl.BlockSpec((1, d, tn), lambda l, j: (l, 0, j)),
                  pl.BlockSpec((1, 1, tn), lambda l, j: (l, 0, j))],
        out_specs=pl.BlockSpec((1, 16, tn), lambda l, j: (l, 0, j)),
        out_shape=jax.ShapeDtypeStruct((nl, 16, n), F32),
        compiler_params=_cp("parallel", "parallel"),
        name="modulation",
    )(c16, w_mod, b_mod.reshape(nl, 1, n))


def _inproj_kernel(x_ref, g_ref, sh_ref, sc_ref, w_ref, o_ref, xn_ref):
    @pl.when(pl.program_id(1) == 0)
    def _():
        y = _rms(x_ref[...], g_ref[...])
        xn_ref[...] = _mod_rows(y, 1.0 + sc_ref[...], sh_ref[...]).astype(BF16)

    o_ref[...] = _bdot(xn_ref[...], w_ref[...])


def in_proj(x, g, mod, mrow, w):
    m, d = x.shape
    n = w.shape[1]
    tn = 1024
    tm = min(1024, m)
    return pl.pallas_call(
        _inproj_kernel,
        grid=(m // tm, n // tn),
        in_specs=[pl.BlockSpec((tm, d), lambda i, j: (i, 0)),
                  pl.BlockSpec((1, d), lambda i, j: (0, 0)),
                  pl.BlockSpec((NB, d), lambda i, j: (mrow, 0)),
                  pl.BlockSpec((NB, d), lambda i, j: (mrow, 1)),
                  pl.BlockSpec((d, tn), lambda i, j: (0, j))],
        out_specs=pl.BlockSpec((tm, tn), lambda i, j: (i, j)),
        out_shape=jax.ShapeDtypeStruct((m, n), F32),
        scratch_shapes=[pltpu.VMEM((tm, d), BF16)],
        compiler_params=_cp("parallel", "arbitrary"),
        name="in_proj",
    )(x, g, mod, mod, w)


def _expm1_neg(x):
    poly = x * (1.0 + x * (0.5 + x * (1.0 / 6 + x * (1.0 / 24 + x * (1.0 / 120 + x * (1.0 / 720))))))
    return jnp.where(x > -0.1, poly, jnp.exp(x) - 1.0)


def _rglru_kernel(*refs, rev, final, tt, cb, n_t):
    if final:
        (xm, xp, xn, rg, hf, cw, cbias, wa, wx, ba, bx, lam, h0, out, hlast, ext_s, a_s, g_s, h_s) = refs
    else:
        (xm, xp, xn, cw, cbias, wa, wx, ba, bx, lam, h0, out, hlast, ext_s, a_s, g_s, h_s) = refs
    r = tt * NB
    i = pl.program_id(1)
    ti = (n_t - 1 - i) if rev else i

    @pl.when(i == 0)
    def _():
        h_s[...] = h0[...]

    ext_s[0:16, :] = jnp.where(ti == 0, 0.0, xp[...])
    ext_s[16:16 + r, :] = xm[...]
    ext_s[16 + r:24 + r, :] = jnp.where(ti == n_t - 1, 0.0, xn[...])
    lam_v = lam[...]
    sp = jnp.maximum(-lam_v, 0.0) + jnp.log(1.0 + jnp.exp(-jnp.abs(lam_v)))
    sub = 128
    for rc in range(r // sub):
        r0 = rc * sub
        for k in range(cb // LRU_BW):
            ls = slice(k * LRU_BW, (k + 1) * LRU_BW)
            u = cbias[:, ls]
            for j in range(4):
                u = u + cw[j:j + 1, ls] * ext_s[r0 + 8 * j:r0 + 8 * j + sub, ls]
            ub = u.astype(BF16)
            rr = jax.nn.sigmoid(_bdot(ub, wa[k]) + ba[:, ls])
            ii = jax.nn.sigmoid(_bdot(ub, wx[k]) + bx[:, ls])
            la = (-LRU_C) * rr * sp[:, ls]
            a_s[r0:r0 + sub, ls] = jnp.exp(la)
            g_s[r0:r0 + sub, ls] = jnp.sqrt(-_expm1_neg(2.0 * la)) * (ii * u)

    def step(s, h):
        t = (tt - 1 - s) if rev else s
        rows = pl.ds(pl.multiple_of(t * NB, NB), NB)
        h = a_s[rows, :] * h + g_s[rows, :]
        g_s[rows, :] = h
        return h

    h = lax.fori_loop(0, tt, step, h_s[...], unroll=8)
    h_s[...] = h
    hlast[...] = h
    if final:
        for rc in range(r // sub):
            rs = slice(rc * sub, (rc + 1) * sub)
            out[rs, :] = ((hf[rs, :] + g_s[rs, :]) * jax.nn.gelu(rg[rs, :])).astype(out.dtype)
    else:
        out[...] = g_s[...]


def rglru_pass(proj, n, d_idx, rev, h0, p, hf=None):
    final = hf is not None
    tt = min(64, n)
    cb = D
    r = tt * NB
    n_t = n // tt
    nc = D // cb

    def tix(i):
        return (n_t - 1 - i) if rev else i

    main = pl.BlockSpec((r, cb), lambda c, i: (tix(i), c))
    prev = pl.BlockSpec((16, cb), lambda c, i: (jnp.maximum(tix(i) * (r // 16) - 1, 0), c))
    nxt = pl.BlockSpec((NB, cb), lambda c, i: (jnp.minimum((tix(i) + 1) * tt, n - 1), c))
    vec = pl.BlockSpec((1, cb), lambda c, i: (0, c))
    wsp = pl.BlockSpec((cb // LRU_BW, LRU_BW, LRU_BW), lambda c, i: (c, 0, 0))
    in_specs = [main, prev, nxt]
    args = [proj, proj, proj]
    if final:
        in_specs += [pl.BlockSpec((r, cb), lambda c, i: (tix(i), OFF_RG // cb + c)), main]
        args += [proj, hf]
    in_specs += [pl.BlockSpec((4, cb), lambda c, i: (0, c)), vec, wsp, wsp, vec, vec, vec,
                 pl.BlockSpec((NB, cb), lambda c, i: (0, c))]
    args += [p["conv_w"], p["conv_b"], p["wa"][d_idx], p["wx"][d_idx], p["ba"][d_idx], p["bx"][d_idx],
             p["lam"][d_idx], h0]
    out, hlast = pl.pallas_call(
        functools.partial(_rglru_kernel, rev=rev, final=final, tt=tt, cb=cb, n_t=n_t),
        grid=(nc, n_t),
        in_specs=in_specs,
        out_specs=[main, pl.BlockSpec((NB, cb), lambda c, i: (0, c))],
        out_shape=[jax.ShapeDtypeStruct((n * NB, D), BF16 if final else F32),
                   jax.ShapeDtypeStruct((NB, D), F32)],
        scratch_shapes=[pltpu.VMEM((r + 24, cb), F32), pltpu.VMEM((r, cb), F32), pltpu.VMEM((r, cb), F32),
                        pltpu.VMEM((NB, cb), F32)],
        compiler_params=_cp("parallel", "arbitrary"),
        name="rglru_bwd" if rev else "rglru_fwd",
    )(*args)
    return out, hlast


TT = 64


def _row_perm(to_batch_major):
    r = TT * NB
    src = np.arange(r)
    t, b = src // NB, src % NB
    p = np.zeros((r, r), np.float32)
    p[b * TT + t, src] = 1.0
    return jnp.asarray(p if to_batch_major else p.T, BF16)


def _hyshort_kernel(xm, xp, xn, w, b, perm, o, *, n_t):
    i = pl.program_id(0)
    r = xm.shape[0]
    tt = r // NB
    prev = jnp.where(i == 0, 0.0, xp[...])
    nxt = jnp.where(i == n_t - 1, 0.0, xn[...])
    x = xm[...]
    sh_prev = jnp.concatenate([prev, x[:r - NB]], axis=0)
    sh_next = jnp.concatenate([x[NB:], nxt], axis=0)
    u = w[0:1, :] * sh_prev + w[1:2, :] * x + w[2:3, :] * sh_next + b[...]
    ub = _bdot(perm[...], u.astype(BF16))
    for bb in range(NB):
        o[:, bb * D:(bb + 1) * D] = ub[bb * tt:(bb + 1) * tt].astype(o.dtype)


def hyena_short(proj, n, w, b):
    tt = min(TT, n)
    assert tt == TT
    r = tt * NB
    n_t = n // tt
    c0 = OFF_HY // D
    return pl.pallas_call(
        functools.partial(_hyshort_kernel, n_t=n_t),
        grid=(n_t, 3),
        in_specs=[pl.BlockSpec((r, D), lambda i, c: (i, c0 + c)),
                  pl.BlockSpec((NB, D), lambda i, c: (jnp.maximum(i * tt - 1, 0), c0 + c)),
                  pl.BlockSpec((NB, D), lambda i, c: (jnp.minimum((i + 1) * tt, n - 1), c0 + c)),
                  pl.BlockSpec((3, D), lambda i, c: (0, c)),
                  pl.BlockSpec((1, D), lambda i, c: (0, c)),
                  pl.BlockSpec((r, r), lambda i, c: (0, 0))],
        out_specs=pl.BlockSpec((tt, NB * D), lambda i, c: (i, c)),
        out_shape=jax.ShapeDtypeStruct((n, 3 * NB * D), BF16),
        compiler_params=_cp("parallel", "parallel"),
        name="hyena_short",
    )(proj, proj, proj, w, b, _row_perm(True))


def _hyfilt_kernel(emb, w1, b1, w2, b2, fr, w3f, w3b, dcf, dcb, of, ob, h_s):
    n = emb.shape[0]

    @pl.when(pl.program_id(0) == 0)
    def _():
        f = fr[...]
        h1 = jnp.sin(f * (jnp.dot(emb[...], w1[...], precision=HI, preferred_element_type=F32) + b1[...]))
        h_s[...] = jnp.sin(f * (jnp.dot(h1, w2[...], precision=HI, preferred_element_type=F32) + b2[...]))

    h = h_s[...]
    row = lax.broadcasted_iota(jnp.int32, (n, 1), 0)
    t = row.astype(F32) * (1.0 / (n - 1))
    kf = jnp.dot(h, w3f[...], precision=HI, preferred_element_type=F32) * jnp.exp(-t * dcf[...])
    kb = jnp.dot(h, w3b[...], precision=HI, preferred_element_type=F32) * jnp.exp(-t * dcb[...])
    kb = jnp.where(row == 0, 0.0, kb)
    nrm = jnp.sum(jnp.abs(kf), axis=0, keepdims=True) + jnp.sum(jnp.abs(kb), axis=0, keepdims=True)
    inv = 1.0 / nrm
    of[...] = (kf * inv).astype(of.dtype)
    ob[...] = (kb * inv).astype(ob.dtype)


def hyena_filters(emb, p):
    n = emb.shape[0]
    tn = 256
    nj = 2 * D // tn
    full = lambda s: pl.BlockSpec(s, lambda j: (0, 0))
    return pl.pallas_call(
        _hyfilt_kernel,
        grid=(nj,),
        in_specs=[full((n, HY_PAD)), full((HY_PAD, HY_PAD)), full((1, HY_PAD)), full((HY_PAD, HY_PAD)),
                  full((1, HY_PAD)), full((1, HY_PAD)),
                  pl.BlockSpec((HY_PAD, tn), lambda j: (0, j)), pl.BlockSpec((HY_PAD, tn), lambda j: (0, nj + j)),
                  pl.BlockSpec((1, tn), lambda j: (0, j)), pl.BlockSpec((1, tn), lambda j: (0, nj + j))],
        out_specs=[pl.BlockSpec((n, tn), lambda j: (0, j)), pl.BlockSpec((n, tn), lambda j: (0, j))],
        out_shape=[jax.ShapeDtypeStruct((n, 2 * D), BF16), jax.ShapeDtypeStruct((n, 2 * D), BF16)],
        scratch_shapes=[pltpu.VMEM((n, HY_PAD), F32)],
        compiler_params=_cp("arbitrary"),
        name="hyena_filters",
    )(emb, p["w1"], p["b1"], p["w2"], p["b2"], p["freq"], p["w3"], p["w3"], p["decay"], p["decay"])


def _dftp_kernel(c_ref, s_ref, x_ref, a_ref, b_ref):
    x = x_ref[...]
    a_ref[...] = _bdot(c_ref[...], x)
    b_ref[...] = _bdot(s_ref[...], x)


def _kcombine_kernel(af_ref, ab_ref, bf_ref, bb_ref, kre_ref, kim_ref):
    i = pl.program_id(0)
    kre_ref[...] = af_ref[...] + ab_ref[...]
    bf, bb = bf_ref[...], bb_ref[...]
    first = (lax.broadcasted_iota(jnp.int32, bf.shape, 0) == 0) & (i == 0)
    kim_ref[...] = jnp.where(first, bf + bb, bb - bf)


def hyena_kspec(tabs, kf, kb):
    cmat, smat, _ = tabs
    n = cmat.shape[0]
    tf = min(1024, n)
    tn = 512
    x = jnp.concatenate([kf, kb], axis=1)
    nj = x.shape[1] // D
    out = pl.BlockSpec((tf, tn), lambda i, j: (i, j))
    a, b = pl.pallas_call(
        _dftp_kernel,
        grid=(n // tf, x.shape[1] // tn),
        in_specs=[_table_spec(tf, n), _table_spec(tf, n), pl.BlockSpec((n, tn), lambda i, j: (0, j))],
        out_specs=[out, out],
        out_shape=[jax.ShapeDtypeStruct(x.shape, F32), jax.ShapeDtypeStruct(x.shape, F32)],
        compiler_params=_cp("parallel", "arbitrary"),
        name="hyena_kdft",
    )(cmat, smat, x)
    fwd = pl.BlockSpec((tf, D), lambda i, j: (i, j))
    bwd = pl.BlockSpec((tf, D), lambda i, j: (i, nj // 2 + j))
    return pl.pallas_call(
        _kcombine_kernel,
        grid=(n // tf, nj // 2),
        in_specs=[fwd, bwd, fwd, bwd],
        out_specs=[fwd, fwd],
        out_shape=[jax.ShapeDtypeStruct((n, 2 * D), F32), jax.ShapeDtypeStruct((n, 2 * D), F32)],
        compiler_params=_cp("parallel", "parallel"),
        name="hyena_kcombine",
    )(a, a, b, b)


def _dftf_kernel(c_ref, s_ref, x_ref, kre_ref, kim_ref, qa_ref, qb_ref, *, inv_n):
    i = pl.program_id(0)
    x = x_ref[...]
    a = _bdot(c_ref[...], x)
    b = _bdot(s_ref[...], x)
    kre, kim = kre_ref[...], kim_ref[...]
    first = (lax.broadcasted_iota(jnp.int32, a.shape, 0) == 0) & (i == 0)
    qa = jnp.where(first, inv_n * (a * kre), (2.0 * inv_n) * (a * kre + b * kim))
    qb = jnp.where(first, inv_n * (b * kim), (2.0 * inv_n) * (b * kre - a * kim))
    qa_ref[...] = qa.astype(qa_ref.dtype)
    qb_ref[...] = qb.astype(qb_ref.dtype)


def _table_spec(tf, n):
    return pl.BlockSpec((tf, n), lambda i, j: (i, 0), pipeline_mode=pl.Buffered(1))


def hyena_fwd(tabs, x2d, part, kre, kim, order):
    cmat, smat, _ = tabs
    n = cmat.shape[0]
    tf = min(1024, n)
    tn = 512
    nb = D // tn
    out = pl.BlockSpec((tf, tn), lambda i, j: (i, j))
    kspec = pl.BlockSpec((tf, tn), lambda i, j: (i, order * nb + j % nb))
    return pl.pallas_call(
        functools.partial(_dftf_kernel, inv_n=1.0 / (2 * n)),
        grid=(n // tf, NB * nb),
        in_specs=[_table_spec(tf, n), _table_spec(tf, n),
                  pl.BlockSpec((n, tn), lambda i, j: (0, part * NB * nb + j)),
                  kspec, kspec],
        out_specs=[out, out],
        out_shape=[jax.ShapeDtypeStruct((n, NB * D), BF16), jax.ShapeDtypeStruct((n, NB * D), BF16)],
        compiler_params=_cp("parallel", "arbitrary"),
        name="hyena_fwd",
    )(cmat, smat, x2d, kre, kim)


def _dfti_kernel(c_ref, st_ref, qa_ref, qb_ref, xk_ref, z_ref, bias_ref, *o_refs):
    y = _bdot(c_ref[...], qa_ref[...]) + _bdot(st_ref[...], qb_ref[...])
    o = xk_ref[...].astype(F32) * (y + bias_ref[0] * z_ref[...].astype(F32))
    for o_ref in o_refs:
        o_ref[...] = o.astype(o_ref.dtype)


def hyena_inv(tabs, qa, qb, u2d, xpart, z2d, zpart, bias, order, out_dtypes):
    cmat, _, smat_t = tabs
    n = cmat.shape[0]
    tf = min(1024, n)
    tn = 256
    nb = D // tn
    out = pl.BlockSpec((tf, tn), lambda i, j: (i, j))
    q = pl.BlockSpec((n, tn), lambda i, j: (0, j))
    return pl.pallas_call(
        _dfti_kernel,
        grid=(n // tf, NB * nb),
        in_specs=[_table_spec(tf, n), _table_spec(tf, n), q, q,
                  pl.BlockSpec((tf, tn), lambda i, j: (i, xpart * NB * nb + j)),
                  pl.BlockSpec((tf, tn), lambda i, j: (i, zpart * NB * nb + j)),
                  pl.BlockSpec((1, 1, tn), lambda i, j: (order, 0, j % nb))],
        out_specs=[out] * len(out_dtypes),
        out_shape=[jax.ShapeDtypeStruct((n, NB * D), dt) for dt in out_dtypes],
        compiler_params=_cp("parallel", "arbitrary"),
        name="hyena_inv",
    )(cmat, smat_t, qa, qb, u2d, z2d, bias)


DFT_R = 64


def _dft_table_kernel(ac_ref, as_ref, bc_ref, bs_ref, c_ref, s_ref, st_ref):
    f1 = pl.program_id(0)
    ac, as_ = ac_ref[0], as_ref[0]
    bc, bs = bc_ref[...], bs_ref[...]
    c = ac * bc - as_ * bs
    s = as_ * bc + ac * bs
    row = lax.broadcasted_iota(jnp.int32, c.shape, 0)
    col = lax.broadcasted_iota(jnp.int32, c.shape, 1)
    alt_col = jnp.where(col % 2 == 0, 1.0, -1.0)
    alt_row = jnp.where(row % 2 == 0, 1.0, -1.0)
    c_ref[...] = c.astype(c_ref.dtype)
    s_ref[...] = jnp.where((row == 0) & (f1 == 0), alt_col, s).astype(s_ref.dtype)
    st_ref[...] = jnp.where(col == 0, alt_row, s).astype(st_ref.dtype)


def dft_tables(n):
    r = min(DFT_R, n)
    t = np.arange(n, dtype=np.int64)[None, :]
    f1 = np.arange(n // r, dtype=np.int64)[:, None]
    f0 = np.arange(r, dtype=np.int64)[:, None]
    ang_a = ((r * f1 * t) % (2 * n)).astype(np.float64) * (np.pi / n)
    ang_b = ((f0 * t) % (2 * n)).astype(np.float64) * (np.pi / n)
    ac = jnp.asarray(np.cos(ang_a), F32).reshape(n // r, 1, n)
    as_ = jnp.asarray(np.sin(ang_a), F32).reshape(n // r, 1, n)
    bc = jnp.asarray(np.cos(ang_b), F32)
    bs = jnp.asarray(np.sin(ang_b), F32)
    arow = pl.BlockSpec((1, 1, n), lambda i: (i, 0, 0))
    bfull = pl.BlockSpec((r, n), lambda i: (0, 0))
    out = pl.BlockSpec((r, n), lambda i: (i, 0))
    return tuple(pl.pallas_call(
        _dft_table_kernel,
        grid=(n // r,),
        in_specs=[arow, arow, bfull, bfull],
        out_specs=[out, out, out],
        out_shape=[jax.ShapeDtypeStruct((n, n), BF16)] * 3,
        compiler_params=_cp("parallel"),
        name="dft_tables",
    )(ac, as_, bc, bs))


def hyena_embedding(n):
    t = np.linspace(0.0, 1.0, n)[:, None]
    bands = np.linspace(1e-4, HY_BANDS - 1, HY_BANDS)
    ang = (2.0 * math.pi / n) * np.arange(n, dtype=np.float64)[:, None] * bands[None]
    emb = np.concatenate([t, np.cos(ang), np.sin(ang)], axis=-1)
    return jnp.asarray(np.pad(emb, ((0, 0), (0, HY_PAD - emb.shape[1]))), F32)


def hyena_mixer(proj, n, tabs, emb, p):
    u2 = hyena_short(proj, n, p["short_w"], p["short_b"])
    kf, kb = hyena_filters(emb, p)
    kre, kim = hyena_kspec(tabs, kf, kb)
    qa, qb = hyena_fwd(tabs, u2, 0, kre, kim, 0)
    (z2,) = hyena_inv(tabs, qa, qb, u2, 1, u2, 0, p["bias"], 0, (BF16,))
    qa, qb = hyena_fwd(tabs, z2, 0, kre, kim, 1)
    (y,) = hyena_inv(tabs, qa, qb, u2, 2, z2, 0, p["bias"], 1, (BF16,))
    return y


def _softplus(x):
    return jnp.maximum(x, 0.0) + jnp.log(1.0 + jnp.exp(-jnp.abs(x)))


def _ssd_kernel(*refs, rev, nc):
    if rev:
        (xm, xp, xn, dt_ref, yprev, cw, cbias, dtb, arow, xexp, h0, y_ref, hlast, ext_s, s_s) = refs
    else:
        (xm, xp, xn, dt_ref, cw, cbias, dtb, arow, dsk, xexp, h0, y_ref, hlast, ext_s, s_s) = refs
    q = SSM_Q
    i = pl.program_id(1)
    ci = (nc - 1 - i) if rev else i

    @pl.when(i == 0)
    def _():
        s_s[...] = h0[0]

    ext_s[0:8, :] = jnp.where(ci == 0, 0.0, xp[0])
    ext_s[8:8 + q, :] = xm[0]
    ext_s[8 + q:16 + q, :] = jnp.where(ci == nc - 1, 0.0, xn[0])
    acc = cbias[...]
    for j in range(4):
        acc = acc + cw[j:j + 1, :] * ext_s[pl.ds(6 + j, q), :]
    xbc = acc * jax.nn.sigmoid(acc)
    xs = xbc[:, :D]
    dtp = _softplus(dt_ref[0] + dtb[...])
    dta = dtp * arow[...]
    row = lax.broadcasted_iota(jnp.int32, (q, q), 0)
    col = lax.broadcasted_iota(jnp.int32, (q, q), 1)
    mask = (row <= col) if rev else (row >= col)
    cum = jnp.dot(mask.astype(F32), dta, precision=HI, preferred_element_type=F32)
    cum_t = cum.T
    cum_e = jnp.dot(cum, xexp[...], precision=HI, preferred_element_type=F32)
    dt_e = jnp.dot(dtp, xexp[...], precision=HI, preferred_element_type=F32)
    last = 0 if rev else q - 1
    cum_last = cum_e[last:last + 1, :]
    xdt = xs * dt_e
    xdt_b = xdt.astype(BF16)
    s_old = s_s[...]
    s_old_b = s_old.astype(BF16)
    w_b = (jnp.exp(cum_last - cum_e) * xdt).astype(BF16)
    lane = lax.broadcasted_iota(jnp.int32, (q, 2 * SSM_P), 1)
    gw = D // SSM_G
    e_per_g = SSM_H // SSM_G
    slot0 = SSM_H if rev else 0
    y_cols = []
    s_cols = []
    for g in range(SSM_G):
        bg = xbc[:, D + g * SSM_N:D + (g + 1) * SSM_N].astype(BF16)
        cg = xbc[:, D + SSM_G * SSM_N + g * SSM_N:D + SSM_G * SSM_N + (g + 1) * SSM_N].astype(BF16)
        cbm = lax.dot_general(cg, bg, (((1,), (1,)), ((), ())), preferred_element_type=F32)
        for pr in range(e_per_g // 2):
            c0 = g * gw + pr * 2 * SSM_P
            xpair = xdt[:, c0:c0 + 2 * SSM_P]
            yp = jnp.zeros((q, 2 * SSM_P), F32)
            for hh in range(2):
                slot = slot0 + g * e_per_g + pr * 2 + hh
                seg = cum[:, slot:slot + 1] - cum_t[slot:slot + 1, :]
                dec = jnp.exp(jnp.where(mask, seg, -jnp.inf))
                mh = (cbm * dec).astype(BF16)
                in_head = (lane >= hh * SSM_P) & (lane < (hh + 1) * SSM_P)
                yp = yp + _bdot(mh, jnp.where(in_head, xpair, 0.0).astype(BF16))
            y_cols.append(yp)
        gs = slice(g * gw, (g + 1) * gw)
        y_state = _bdot(cg, s_old_b[:, gs])
        y_cols[-(e_per_g // 2):] = [
            y_cols[-(e_per_g // 2) + pr] + jnp.exp(cum_e[:, g * gw + pr * 2 * SSM_P:g * gw + (pr + 1) * 2 * SSM_P])
            * y_state[:, pr * 2 * SSM_P:(pr + 1) * 2 * SSM_P] for pr in range(e_per_g // 2)]
        s_new = jnp.exp(cum_last[:, gs]) * s_old[:, gs] + lax.dot_general(
            bg, w_b[:, gs], (((0,), (0,)), ((), ())), preferred_element_type=F32)
        s_cols.append(s_new)
    y = jnp.concatenate(y_cols, axis=1)
    s_new = jnp.concatenate(s_cols, axis=1)
    s_s[...] = s_new
    hlast[0] = s_new
    if rev:
        y_ref[0] = (y + yprev[0]).astype(y_ref.dtype)
    else:
        y_ref[0] = y + dsk[...] * xs


def _scan_perm(n, col_major, inverse):
    r_ = min(n // GRID_W, TT) if col_major else TT
    s_ = TT // r_
    src = np.arange(TT * NB)
    b, rs = src % NB, src // NB
    r, s = rs // s_, rs % s_
    p = np.zeros((TT * NB, TT * NB), np.float32)
    p[b * TT + s * r_ + r, src] = 1.0
    return jnp.asarray(p.T if inverse else p, BF16)


def _scan_in_kernel(x_ref, perm_ref, o_ref):
    x = x_ref[...].reshape(TT * NB, x_ref.shape[-1])
    hi = x.astype(BF16)
    lo = (x - hi.astype(F32)).astype(BF16)
    p = perm_ref[...]
    y = _bdot(p, hi) + _bdot(p, lo)
    for bb in range(NB):
        o_ref[bb] = y[bb * TT:(bb + 1) * TT]


def _tile_spec(n, col_major, width, cblk):
    if col_major:
        rows = n // GRID_W
        r_ = min(rows, TT)
        assert r_ == rows, "a scan tile must cover whole grid columns"
        return (rows, GRID_W, NB), pl.BlockSpec((rows, TT // r_, NB, width), lambda k: (0, k, 0, cblk))
    return (n // TT, TT, NB), pl.BlockSpec((1, TT, NB, width), lambda k: (k, 0, 0, cblk))


def scan_in(proj, n, col_major):
    lead, spec = _tile_spec(n, col_major, D_SSD, OFF_XBC // D_SSD)
    return pl.pallas_call(
        _scan_in_kernel,
        grid=(n // TT,),
        in_specs=[spec, pl.BlockSpec((TT * NB, TT * NB), lambda k: (0, 0))],
        out_specs=pl.BlockSpec((NB, TT, D_SSD), lambda k: (0, k, 0)),
        out_shape=jax.ShapeDtypeStruct((NB, n, D_SSD), F32),
        compiler_params=_cp("parallel"),
        name="scan_in",
    )(proj.reshape(lead + (proj.shape[-1],)), _scan_perm(n, col_major, False))


def _scan_out_kernel(y_ref, perm_ref, o_ref):
    y = jnp.concatenate([y_ref[bb] for bb in range(NB)], axis=0)
    o_ref[...] = _bdot(perm_ref[...], y).reshape(o_ref.shape)


def scan_out(y, n, col_major):
    lead, spec = _tile_spec(n, col_major, D, 0)
    out = pl.pallas_call(
        _scan_out_kernel,
        grid=(n // TT,),
        in_specs=[pl.BlockSpec((NB, TT, D), lambda k: (0, k, 0)), pl.BlockSpec((TT * NB, TT * NB), lambda k: (0, 0))],
        out_specs=spec,
        out_shape=jax.ShapeDtypeStruct(lead + (D,), F32),
        compiler_params=_cp("parallel"),
        name="scan_out",
    )(y, _scan_perm(n, col_major, True))
    return out.reshape(n * NB, D)


def ssd_pass(xs, rev, h0, p, yprev=None):
    _, n, _ = xs.shape
    q = SSM_Q
    nc = n // q
    d_idx = 1 if rev else 0

    def cix(i):
        return (nc - 1 - i) if rev else i

    in_specs = [pl.BlockSpec((1, q, D_XBC), lambda b, i: (b, cix(i), 0)),
                pl.BlockSpec((1, NB, D_XBC), lambda b, i: (b, jnp.maximum(cix(i) * (q // NB) - 1, 0), 0)),
                pl.BlockSpec((1, NB, D_XBC), lambda b, i: (b, jnp.minimum((cix(i) + 1) * (q // NB), n // NB - 1), 0)),
                pl.BlockSpec((1, q, 128), lambda b, i: (b, cix(i), D_XBC // 128))]
    args = [xs, xs, xs, xs]
    ymain = pl.BlockSpec((1, q, D), lambda b, i: (b, cix(i), 0))
    if rev:
        in_specs.append(ymain)
        args.append(yprev)
    full = lambda s: pl.BlockSpec(s, lambda b, i: (0,) * len(s))
    in_specs += [full((4, D_XBC)), full((1, D_XBC)), full((1, 128)), full((1, 128))]
    args += [p["conv_w"], p["conv_b"], p["dt_bias"], p["arow"][d_idx]]
    if not rev:
        in_specs.append(full((1, D)))
        args.append(p["d_skip"])
    in_specs += [full((128, D)), pl.BlockSpec((1, SSM_N, D), lambda b, i: (b, 0, 0))]
    args += [p["xexp"][d_idx], h0]
    y, hlast = pl.pallas_call(
        functools.partial(_ssd_kernel, rev=rev, nc=nc),
        grid=(NB, nc),
        in_specs=in_specs,
        out_specs=[ymain, pl.BlockSpec((1, SSM_N, D), lambda b, i: (b, 0, 0))],
        out_shape=[jax.ShapeDtypeStruct((NB, n, D), BF16 if rev else F32), jax.ShapeDtypeStruct((NB, SSM_N, D), F32)],
        scratch_shapes=[pltpu.VMEM((q + 16, D_XBC), F32), pltpu.VMEM((SSM_N, D), F32)],
        compiler_params=_cp("parallel", "arbitrary"),
        name="ssd_bwd" if rev else "ssd_fwd",
    )(*args)
    return y, hlast


def ssd_bidir(xs, hf0, hb0, p):
    yf, sf = ssd_pass(xs, False, hf0, p)
    y, sb = ssd_pass(xs, True, hb0, p, yprev=yf)
    return y, sf, sb


def _merge_kernel(x_ref, gr_ref, gh_ref, gs_ref, yr_ref, yh_ref, ys_ref, z_ref, nw_ref, wb_ref, wo_ref, g1_ref,
                  perm_ref, o_ref):
    z = z_ref[...]
    ys = _rms(ys_ref[...].astype(F32) * (z * jax.nn.sigmoid(z)), nw_ref[...])
    m = jax.nn.sigmoid(gr_ref[...]) * _bdot(yr_ref[...], wb_ref[0])
    yh = jnp.concatenate([yh_ref[:, bb * D:(bb + 1) * D] for bb in range(NB)], axis=0)
    yh = _bdot(perm_ref[...], yh).astype(BF16)
    m = m + jax.nn.sigmoid(gh_ref[...]) * _bdot(yh, wb_ref[1])
    m = m + jax.nn.sigmoid(gs_ref[...]) * _bdot(ys.astype(BF16), wb_ref[2])
    o = _bdot(m.astype(BF16), wo_ref[...])
    o_ref[...] = x_ref[...] + _mod_rows(o, g1_ref[...], None)


def merge(x, proj, yr, yh, ys, norm_w, wb, wo, mod, mrow):
    m = x.shape[0]
    row = lambda w, c: pl.BlockSpec((TM, w), lambda i: (i, c))
    return pl.pallas_call(
        _merge_kernel,
        grid=(m // TM,),
        in_specs=[row(D, 0), row(D, OFF_GT // D), row(D, OFF_GT // D + 1), row(D, OFF_GT // D + 2), row(D, 0),
                  pl.BlockSpec((TM // NB, NB * D), lambda i: (i, 0)), row(D, 0), row(D, OFF_SZ // D),
                  pl.BlockSpec((1, D), lambda i: (0, 0)),
                  pl.BlockSpec((3, D, D), lambda i: (0, 0, 0)),
                  pl.BlockSpec((D, D), lambda i: (0, 0)),
                  pl.BlockSpec((NB, D), lambda i: (mrow, 2)),
                  pl.BlockSpec((TM, TM), lambda i: (0, 0))],
        out_specs=row(D, 0),
        out_shape=jax.ShapeDtypeStruct((m, D), F32),
        compiler_params=_cp("parallel"),
        name="merge",
    )(x, proj, proj, proj, yr, yh, ys, proj, norm_w, wb, wo, mod, _row_perm(False))


def _ffn_kernel(x_ref, g_ref, sh_ref, sc_ref, g2_ref, wg_ref, wu_ref, wd_ref, fn_ref, o_ref, xn_s, acc_s, *, nf, final):
    f = pl.program_id(1)

    @pl.when(f == 0)
    def _():
        y = _rms(x_ref[...], g_ref[...])
        xn_s[...] = _mod_rows(y, 1.0 + sc_ref[...], sh_ref[...]).astype(BF16)
        acc_s[...] = jnp.zeros_like(acc_s)

    xn = xn_s[...]
    hg = _bdot(xn, wg_ref[...])
    hu = _bdot(xn, wu_ref[...])
    hh = (hg * jax.nn.sigmoid(hg) * hu).astype(BF16)
    acc_s[...] += _bdot(hh, wd_ref[...])

    @pl.when(f == nf - 1)
    def _():
        o = x_ref[...] + _mod_rows(acc_s[...], g2_ref[...], None)
        if final:
            o = _rms(o, fn_ref[...])
        o_ref[...] = o


def ffn(x, g, mod, mrow, w_up, w_down, final_norm, final):
    m = x.shape[0]
    d_ff = w_down.shape[0]
    nf = 2
    tf = d_ff // nf
    vec = pl.BlockSpec((1, D), lambda i, f: (0, 0))
    return pl.pallas_call(
        functools.partial(_ffn_kernel, nf=nf, final=final),
        grid=(m // TM, nf),
        in_specs=[pl.BlockSpec((TM, D), lambda i, f: (i, 0)), vec,
                  pl.BlockSpec((NB, D), lambda i, f: (mrow, 3)),
                  pl.BlockSpec((NB, D), lambda i, f: (mrow, 4)),
                  pl.BlockSpec((NB, D), lambda i, f: (mrow, 5)),
                  pl.BlockSpec((D, tf), lambda i, f: (0, f)),
                  pl.BlockSpec((D, tf), lambda i, f: (0, nf + f)),
                  pl.BlockSpec((tf, D), lambda i, f: (f, 0)), vec],
        out_specs=pl.BlockSpec((TM, D), lambda i, f: (i, 0)),
        out_shape=jax.ShapeDtypeStruct((m, D), F32),
        scratch_shapes=[pltpu.VMEM((TM, D), BF16), pltpu.VMEM((TM, D), F32)],
        compiler_params=_cp("parallel", "arbitrary"),
        name="ffn",
    )(x, g, mod, mod, mod, w_up, w_up, w_down, final_norm)


def _pack_w_in(w):
    o_sp = 6 * D
    o_gt = o_sp + D_XBC + 2 * SSM_H
    dt = jnp.pad(w[:, o_sp + D_XBC:o_gt], ((0, 0), (0, OFF_GT - OFF_DT - 2 * SSM_H)))
    out = jnp.concatenate([w[:, :o_sp], w[:, o_sp:o_sp + D_XBC], dt, w[:, o_gt:o_gt + 3 * D]], axis=1)
    assert out.shape[1] == D_INP
    return out.astype(BF16)


def _pad_to(a, shape):
    return jnp.pad(a, [(0, s - d) for s, d in zip(shape, a.shape)])


def kernel(x, c, ctx, c_ctx, w_mod, b_mod, norm_mix, norm_ffn, w_in, rnn_conv_w, rnn_conv_b, rnn_gate_a_w, rnn_gate_a_b, rnn_gate_x_w, rnn_gate_x_b, rnn_lambda, hy_short_w, hy_short_b, hy_w1, hy_b1, hy_w2, hy_b2, hy_w3, hy_freq, hy_decay, hy_bias, ssm_conv_w, ssm_conv_b, ssm_a_log, ssm_dt_bias, ssm_d, ssm_norm, w_branch, w_out, w_up, w_down, final_norm):
    bsz, n_l, d = x.shape
    n_c = ctx.shape[1]
    depth = w_in.shape[0]
    assert bsz == NB and d == D

    xl = x.transpose(1, 0, 2).reshape(n_l * NB, D)
    sc = ctx.transpose(1, 0, 2).reshape(n_c * NB, D)
    c16 = jnp.concatenate([c, jnp.broadcast_to(c_ctx[None], (NB, D))], axis=0)
    mods = modulation(c16, w_mod, b_mod)

    tabs = {n_l: dft_tables(n_l), n_c: dft_tables(n_c)}
    embs = {n_l: hyena_embedding(n_l), n_c: hyena_embedding(n_c)}
    head_lane = jnp.arange(128)[:, None]
    hp_head = (jnp.arange(D) // SSM_P)[None, :]
    xexp = jnp.stack([(head_lane == hp_head + SSM_H * dd).astype(F32) for dd in range(2)])
    zeros_h = jnp.zeros((NB, D), F32)
    zeros_s = jnp.zeros((NB, SSM_N, D), F32)
    fn = final_norm.reshape(1, D)

    for i in range(depth):
        ctx_out = i < depth - 1
        mod = mods[i]
        w_i = _pack_w_in(w_in[i])
        rp = dict(conv_w=rnn_conv_w[i], conv_b=rnn_conv_b[i].reshape(1, D),
                  wa=rnn_gate_a_w[i].astype(BF16), wx=rnn_gate_x_w[i].astype(BF16),
                  ba=rnn_gate_a_b[i].reshape(2, 1, D), bx=rnn_gate_x_b[i].reshape(2, 1, D),
                  lam=rnn_lambda[i].reshape(2, 1, D))
        hp = dict(short_w=hy_short_w[i], short_b=hy_short_b[i].reshape(1, 3 * D),
                  w1=_pad_to(hy_w1[i], (HY_PAD, HY_PAD)), b1=_pad_to(hy_b1[i].reshape(1, -1), (1, HY_PAD)),
                  w2=_pad_to(hy_w2[i], (HY_PAD, HY_PAD)), b2=_pad_to(hy_b2[i].reshape(1, -1), (1, HY_PAD)),
                  freq=_pad_to(hy_freq[i].reshape(1, -1), (1, HY_PAD)),
                  w3=_pad_to(hy_w3[i], (HY_PAD, 4 * D)), decay=hy_decay[i].reshape(1, 4 * D),
                  bias=hy_bias[i].reshape(2, 1, D))
        a_neg = -jnp.exp(ssm_a_log[i].astype(F32))
        arow = jnp.stack([_pad_to(jnp.concatenate([jnp.zeros((SSM_H * dd,), F32), a_neg[dd]])[None], (1, 128))
                          for dd in range(2)])
        sp = dict(conv_w=ssm_conv_w[i], conv_b=ssm_conv_b[i].reshape(1, D_XBC),
                  dt_bias=_pad_to(ssm_dt_bias[i].reshape(1, 2 * SSM_H), (1, 128)), arow=arow,
                  d_skip=jnp.repeat(ssm_d[i], SSM_P).reshape(1, D), xexp=xexp)
        wb = w_branch[i].astype(BF16)
        wo = w_out[i].astype(BF16)
        wu = w_up[i].astype(BF16)
        wd = w_down[i].astype(BF16)
        g_mix = norm_mix[i].reshape(1, D)
        g_ffn = norm_ffn[i].reshape(1, D)
        nrm_s = ssm_norm[i].reshape(1, D)

        proj_c = in_proj(sc, g_mix, mod, 1, w_i)
        proj_l = in_proj(xl, g_mix, mod, 0, w_i)

        hf_c, hf_last = rglru_pass(proj_c, n_c, 0, False, zeros_h, rp)
        yr_c, hb_last = rglru_pass(proj_c, n_c, 1, True, zeros_h, rp, hf=hf_c)
        hf_l, _ = rglru_pass(proj_l, n_l, 0, False, hf_last, rp)
        yr_l, _ = rglru_pass(proj_l, n_l, 1, True, hb_last, rp, hf=hf_l)

        ys_c, sf, sb = ssd_bidir(scan_in(proj_c, n_c, False), zeros_s, zeros_s, sp)
        ys_l, _, _ = ssd_bidir(scan_in(proj_l, n_l, True), sf, sb, sp)
        ys_l = scan_out(ys_l, n_l, True)

        yh_l = hyena_mixer(proj_l, n_l, tabs[n_l], embs[n_l], hp)
        xl = merge(xl, proj_l, yr_l, yh_l, ys_l, nrm_s, wb, wo, mod, 0)
        xl = ffn(xl, g_ffn, mod, 0, wu, wd, fn, final=not ctx_out)

        if ctx_out:
            ys_c = scan_out(ys_c, n_c, False)
            yh_c = hyena_mixer(proj_c, n_c, tabs[n_c], embs[n_c], hp)
            sc = merge(sc, proj_c, yr_c, yh_c, ys_c, nrm_s, wb, wo, mod, 1)
            sc = ffn(sc, g_ffn, mod, 1, wu, wd, fn, final=False)

    return xl.reshape(n_l, NB, D).transpose(1, 0, 2)
```

```python
import functools
import math

import numpy as np
import jax
import jax.numpy as jnp
from jax import lax
from jax.experimental import pallas as pl
from jax.experimental.pallas import tpu as pltpu

F32 = jnp.float32
BF16 = jnp.bfloat16
HI = lax.Precision.HIGHEST

NB = 8
GRID_W = 64
NORM_EPS = 1e-6
LRU_C = 8.0
LRU_BW = 128
SSM_P = 64
SSM_H = 16
SSM_G = 2
SSM_N = 128
SSM_Q = 128
HY_BANDS = 16
HY_PAD = 128

D = 1024
OFF_RX, OFF_RG, OFF_HY, OFF_SZ, OFF_XBC, OFF_DT, OFF_GT = 0, 1024, 2048, 5120, 6144, 7680, 8192
D_SSD = 2048
D_XBC = 1536
D_INP = 11264

TM = 512


VMEM_LIMIT = 56 * 1024 * 1024


def _cp(*sem):
    return pltpu.CompilerParams(dimension_semantics=sem, vmem_limit_bytes=VMEM_LIMIT)


def _bdot(a, b):
    return jnp.dot(a, b, preferred_element_type=F32)


def _mod_rows(y, scale8, shift8):
    tm, d = y.shape
    y3 = y.reshape(tm // NB, NB, d)
    if scale8 is not None:
        y3 = y3 * scale8[None]
    if shift8 is not None:
        y3 = y3 + shift8[None]
    return y3.reshape(tm, d)


def _rms(x, g):
    ms = jnp.mean(x * x, axis=-1, keepdims=True)
    return x * lax.rsqrt(ms + NORM_EPS) * g


def _mod_kernel(c_ref, w_ref, b_ref, o_ref):
    c = c_ref[...]
    s = c * jax.nn.sigmoid(c)
    o_ref[0] = jnp.dot(s, w_ref[0], precision=HI, preferred_element_type=F32) + b_ref[0]


def modulation(c16, w_mod, b_mod):
    nl, d, n = w_mod.shape
    tn = 1536
    return pl.pallas_call(
        _mod_kernel,
        grid=(nl, n // tn),
        in_specs=[pl.BlockSpec((16, d), lambda l, j: (0, 0)),
                  pl.BlockSpec((1, d, tn), lambda l, j: (l, 0, j)),
                  pl.BlockSpec((1, 1, tn), lambda l, j: (l, 0, j))],
        out_specs=pl.BlockSpec((1, 16, tn), lambda l, j: (l, 0, j)),
        out_shape=jax.ShapeDtypeStruct((nl, 16, n), F32),
        compiler_params=_cp("parallel", "parallel"),
        name="modulation",
    )(c16, w_mod, b_mod.reshape(nl, 1, n))


def _inproj_kernel(x_ref, g_ref, sh_ref, sc_ref, w_ref, o_ref, xn_ref):
    @pl.when(pl.program_id(1) == 0)
    def _():
        y = _rms(x_ref[...], g_ref[...])
        xn_ref[...] = _mod_rows(y, 1.0 + sc_ref[...], sh_ref[...]).astype(BF16)

    o_ref[...] = _bdot(xn_ref[...], w_ref[...])


def in_proj(x, g, mod, mrow, w):
    m, d = x.shape
    n = w.shape[1]
    tn = 1024
    tm = min(2048, m)
    return pl.pallas_call(
        _inproj_kernel,
        grid=(m // tm, n // tn),
        in_specs=[pl.BlockSpec((tm, d), lambda i, j: (i, 0)),
                  pl.BlockSpec((1, d), lambda i, j: (0, 0)),
                  pl.BlockSpec((NB, d), lambda i, j: (mrow, 0)),
                  pl.BlockSpec((NB, d), lambda i, j: (mrow, 1)),
                  pl.BlockSpec((d, tn), lambda i, j: (0, j))],
        out_specs=pl.BlockSpec((tm, tn), lambda i, j: (i, j)),
        out_shape=jax.ShapeDtypeStruct((m, n), F32),
        scratch_shapes=[pltpu.VMEM((tm, d), BF16)],
        compiler_params=_cp("parallel", "arbitrary"),
        name="in_proj",
    )(x, g, mod, mod, w)


def _expm1_neg(x):
    poly = x * (1.0 + x * (0.5 + x * (1.0 / 6 + x * (1.0 / 24 + x * (1.0 / 120 + x * (1.0 / 720))))))
    return jnp.where(x > -0.1, poly, jnp.exp(x) - 1.0)


def _rglru_kernel(*refs, rev, final, tt, cb, n_t):
    if final:
        (xm, xp, xn, rg, hf, cw, cbias, wa, wx, ba, bx, lam, h0, out, hlast, ext_s, a_s, g_s, h_s) = refs
    else:
        (xm, xp, xn, cw, cbias, wa, wx, ba, bx, lam, h0, out, hlast, ext_s, a_s, g_s, h_s) = refs
    r = tt * NB
    i = pl.program_id(1)
    ti = (n_t - 1 - i) if rev else i

    @pl.when(i == 0)
    def _():
        h_s[...] = h0[...]

    ext_s[0:16, :] = jnp.where(ti == 0, 0.0, xp[...])
    ext_s[16:16 + r, :] = xm[...]
    ext_s[16 + r:24 + r, :] = jnp.where(ti == n_t - 1, 0.0, xn[...])
    lam_v = lam[...]
    sp = jnp.maximum(-lam_v, 0.0) + jnp.log(1.0 + jnp.exp(-jnp.abs(lam_v)))
    sub = 128
    for rc in range(r // sub):
        r0 = rc * sub
        for k in range(cb // LRU_BW):
            ls = slice(k * LRU_BW, (k + 1) * LRU_BW)
            u = cbias[:, ls]
            for j in range(4):
                u = u + cw[j:j + 1, ls] * ext_s[r0 + 8 * j:r0 + 8 * j + sub, ls]
            ub = u.astype(BF16)
            rr = jax.nn.sigmoid(_bdot(ub, wa[k]) + ba[:, ls])
            ii = jax.nn.sigmoid(_bdot(ub, wx[k]) + bx[:, ls])
            la = (-LRU_C) * rr * sp[:, ls]
            a_s[r0:r0 + sub, ls] = jnp.exp(la)
            g_s[r0:r0 + sub, ls] = jnp.sqrt(-_expm1_neg(2.0 * la)) * (ii * u)

    def step(s, h):
        t = (tt - 1 - s) if rev else s
        rows = pl.ds(pl.multiple_of(t * NB, NB), NB)
        h = a_s[rows, :] * h + g_s[rows, :]
        g_s[rows, :] = h
        return h

    h = lax.fori_loop(0, tt, step, h_s[...], unroll=8)
    h_s[...] = h
    hlast[...] = h
    if final:
        for rc in range(r // sub):
            rs = slice(rc * sub, (rc + 1) * sub)
            out[rs, :] = ((hf[rs, :] + g_s[rs, :]) * jax.nn.gelu(rg[rs, :])).astype(out.dtype)
    else:
        out[...] = g_s[...]


def rglru_pass(proj, n, d_idx, rev, h0, p, hf=None):
    final = hf is not None
    tt = min(64, n)
    cb = D
    r = tt * NB
    n_t = n // tt
    nc = D // cb

    def tix(i):
        return (n_t - 1 - i) if rev else i

    main = pl.BlockSpec((r, cb), lambda c, i: (tix(i), c))
    prev = pl.BlockSpec((16, cb), lambda c, i: (jnp.maximum(tix(i) * (r // 16) - 1, 0), c))
    nxt = pl.BlockSpec((NB, cb), lambda c, i: (jnp.minimum((tix(i) + 1) * tt, n - 1), c))
    vec = pl.BlockSpec((1, cb), lambda c, i: (0, c))
    wsp = pl.BlockSpec((cb // LRU_BW, LRU_BW, LRU_BW), lambda c, i: (c, 0, 0))
    in_specs = [main, prev, nxt]
    args = [proj, proj, proj]
    if final:
        in_specs += [pl.BlockSpec((r, cb), lambda c, i: (tix(i), OFF_RG // cb + c)), main]
        args += [proj, hf]
    in_specs += [pl.BlockSpec((4, cb), lambda c, i: (0, c)), vec, wsp, wsp, vec, vec, vec,
                 pl.BlockSpec((NB, cb), lambda c, i: (0, c))]
    args += [p["conv_w"], p["conv_b"], p["wa"][d_idx], p["wx"][d_idx], p["ba"][d_idx], p["bx"][d_idx],
             p["lam"][d_idx], h0]
    out, hlast = pl.pallas_call(
        functools.partial(_rglru_kernel, rev=rev, final=final, tt=tt, cb=cb, n_t=n_t),
        grid=(nc, n_t),
        in_specs=in_specs,
        out_specs=[main, pl.BlockSpec((NB, cb), lambda c, i: (0, c))],
        out_shape=[jax.ShapeDtypeStruct((n * NB, D), BF16 if final else F32),
                   jax.ShapeDtypeStruct((NB, D), F32)],
        scratch_shapes=[pltpu.VMEM((r + 24, cb), F32), pltpu.VMEM((r, cb), F32), pltpu.VMEM((r, cb), F32),
                        pltpu.VMEM((NB, cb), F32)],
        compiler_params=_cp("parallel", "arbitrary"),
        name="rglru_bwd" if rev else "rglru_fwd",
    )(*args)
    return out, hlast


TT = 64


def _row_perm(to_batch_major):
    r = TT * NB
    src = np.arange(r)
    t, b = src // NB, src % NB
    p = np.zeros((r, r), np.float32)
    p[b * TT + t, src] = 1.0
    return jnp.asarray(p if to_batch_major else p.T, BF16)


def _hyshort_kernel(xm, xp, xn, w, b, perm, o, *, n_t):
    i = pl.program_id(0)
    r = xm.shape[0]
    tt = r // NB
    prev = jnp.where(i == 0, 0.0, xp[...])
    nxt = jnp.where(i == n_t - 1, 0.0, xn[...])
    x = xm[...]
    sh_prev = jnp.concatenate([prev, x[:r - NB]], axis=0)
    sh_next = jnp.concatenate([x[NB:], nxt], axis=0)
    u = w[0:1, :] * sh_prev + w[1:2, :] * x + w[2:3, :] * sh_next + b[...]
    ub = _bdot(perm[...], u.astype(BF16))
    for bb in range(NB):
        o[:, bb * D:(bb + 1) * D] = ub[bb * tt:(bb + 1) * tt].astype(o.dtype)


def hyena_short(proj, n, w, b):
    tt = min(TT, n)
    assert tt == TT
    r = tt * NB
    n_t = n // tt
    c0 = OFF_HY // D
    return pl.pallas_call(
        functools.partial(_hyshort_kernel, n_t=n_t),
        grid=(n_t, 3),
        in_specs=[pl.BlockSpec((r, D), lambda i, c: (i, c0 + c)),
                  pl.BlockSpec((NB, D), lambda i, c: (jnp.maximum(i * tt - 1, 0), c0 + c)),
                  pl.BlockSpec((NB, D), lambda i, c: (jnp.minimum((i + 1) * tt, n - 1), c0 + c)),
                  pl.BlockSpec((3, D), lambda i, c: (0, c)),
                  pl.BlockSpec((1, D), lambda i, c: (0, c)),
                  pl.BlockSpec((r, r), lambda i, c: (0, 0))],
        out_specs=pl.BlockSpec((tt, NB * D), lambda i, c: (i, c)),
        out_shape=jax.ShapeDtypeStruct((n, 3 * NB * D), BF16),
        compiler_params=_cp("parallel", "parallel"),
        name="hyena_short",
    )(proj, proj, proj, w, b, _row_perm(True))


def _hyfilt_kernel(emb, w1, b1, w2, b2, fr, w3f, w3b, dcf, dcb, of, ob, h_s):
    n = emb.shape[0]

    @pl.when(pl.program_id(0) == 0)
    def _():
        f = fr[...]
        h1 = jnp.sin(f * (jnp.dot(emb[...], w1[...], precision=HI, preferred_element_type=F32) + b1[...]))
        h_s[...] = jnp.sin(f * (jnp.dot(h1, w2[...], precision=HI, preferred_element_type=F32) + b2[...]))

    h = h_s[...]
    row = lax.broadcasted_iota(jnp.int32, (n, 1), 0)
    t = row.astype(F32) * (1.0 / (n - 1))
    kf = jnp.dot(h, w3f[...], precision=HI, preferred_element_type=F32) * jnp.exp(-t * dcf[...])
    kb = jnp.dot(h, w3b[...], precision=HI, preferred_element_type=F32) * jnp.exp(-t * dcb[...])
    kb = jnp.where(row == 0, 0.0, kb)
    nrm = jnp.sum(jnp.abs(kf), axis=0, keepdims=True) + jnp.sum(jnp.abs(kb), axis=0, keepdims=True)
    inv = 1.0 / nrm
    of[...] = (kf * inv).astype(of.dtype)
    ob[...] = (kb * inv).astype(ob.dtype)


def hyena_filters(emb, p):
    n = emb.shape[0]
    tn = 256
    nj = 2 * D // tn
    full = lambda s: pl.BlockSpec(s, lambda j: (0, 0))
    return pl.pallas_call(
        _hyfilt_kernel,
        grid=(nj,),
        in_specs=[full((n, HY_PAD)), full((HY_PAD, HY_PAD)), full((1, HY_PAD)), full((HY_PAD, HY_PAD)),
                  full((1, HY_PAD)), full((1, HY_PAD)),
                  pl.BlockSpec((HY_PAD, tn), lambda j: (0, j)), pl.BlockSpec((HY_PAD, tn), lambda j: (0, nj + j)),
                  pl.BlockSpec((1, tn), lambda j: (0, j)), pl.BlockSpec((1, tn), lambda j: (0, nj + j))],
        out_specs=[pl.BlockSpec((n, tn), lambda j: (0, j)), pl.BlockSpec((n, tn), lambda j: (0, j))],
        out_shape=[jax.ShapeDtypeStruct((n, 2 * D), BF16), jax.ShapeDtypeStruct((n, 2 * D), BF16)],
        scratch_shapes=[pltpu.VMEM((n, HY_PAD), F32)],
        compiler_params=_cp("arbitrary"),
        name="hyena_filters",
    )(emb, p["w1"], p["b1"], p["w2"], p["b2"], p["freq"], p["w3"], p["w3"], p["decay"], p["decay"])


def _dftp_kernel(c_ref, s_ref, x_ref, a_ref, b_ref):
    x = x_ref[...]
    a_ref[...] = _bdot(c_ref[...], x)
    b_ref[...] = _bdot(s_ref[...], x)


def _kcombine_kernel(af_ref, ab_ref, bf_ref, bb_ref, kre_ref, kim_ref):
    i = pl.program_id(0)
    kre_ref[...] = af_ref[...] + ab_ref[...]
    bf, bb = bf_ref[...], bb_ref[...]
    first = (lax.broadcasted_iota(jnp.int32, bf.shape, 0) == 0) & (i == 0)
    kim_ref[...] = jnp.where(first, bf + bb, bb - bf)


def hyena_kspec(tabs, kf, kb):
    cmat, smat, _ = tabs
    n = cmat.shape[0]
    tf = min(1024, n)
    tn = 512
    x = jnp.concatenate([kf, kb], axis=1)
    nj = x.shape[1] // D
    out = pl.BlockSpec((tf, tn), lambda i, j: (i, j))
    a, b = pl.pallas_call(
        _dftp_kernel,
        grid=(n // tf, x.shape[1] // tn),
        in_specs=[_table_spec(tf, n), _table_spec(tf, n), pl.BlockSpec((n, tn), lambda i, j: (0, j))],
        out_specs=[out, out],
        out_shape=[jax.ShapeDtypeStruct(x.shape, F32), jax.ShapeDtypeStruct(x.shape, F32)],
        compiler_params=_cp("parallel", "arbitrary"),
        name="hyena_kdft",
    )(cmat, smat, x)
    fwd = pl.BlockSpec((tf, D), lambda i, j: (i, j))
    bwd = pl.BlockSpec((tf, D), lambda i, j: (i, nj // 2 + j))
    return pl.pallas_call(
        _kcombine_kernel,
        grid=(n // tf, nj // 2),
        in_specs=[fwd, bwd, fwd, bwd],
        out_specs=[fwd, fwd],
        out_shape=[jax.ShapeDtypeStruct((n, 2 * D), F32), jax.ShapeDtypeStruct((n, 2 * D), F32)],
        compiler_params=_cp("parallel", "parallel"),
        name="hyena_kcombine",
    )(a, a, b, b)


def _dftf_kernel(c_ref, s_ref, x_ref, kre_ref, kim_ref, qa_ref, qb_ref, *, inv_n):
    i = pl.program_id(0)
    x = x_ref[...]
    a = _bdot(c_ref[...], x)
    b = _bdot(s_ref[...], x)
    kre, kim = kre_ref[...], kim_ref[...]
    first = (lax.broadcasted_iota(jnp.int32, a.shape, 0) == 0) & (i == 0)
    qa = jnp.where(first, inv_n * (a * kre), (2.0 * inv_n) * (a * kre + b * kim))
    qb = jnp.where(first, inv_n * (b * kim), (2.0 * inv_n) * (b * kre - a * kim))
    qa_ref[...] = qa.astype(qa_ref.dtype)
    qb_ref[...] = qb.astype(qb_ref.dtype)


def _table_spec(tf, n):
    return pl.BlockSpec((tf, n), lambda i, j: (i, 0), pipeline_mode=pl.Buffered(1))


def hyena_fwd(tabs, x2d, part, kre, kim, order):
    cmat, smat, _ = tabs
    n = cmat.shape[0]
    tf = min(1024, n)
    tn = 512
    nb = D // tn
    out = pl.BlockSpec((tf, tn), lambda i, j: (i, j))
    kspec = pl.BlockSpec((tf, tn), lambda i, j: (i, order * nb + j % nb))
    return pl.pallas_call(
        functools.partial(_dftf_kernel, inv_n=1.0 / (2 * n)),
        grid=(n // tf, NB * nb),
        in_specs=[_table_spec(tf, n), _table_spec(tf, n),
                  pl.BlockSpec((n, tn), lambda i, j: (0, part * NB * nb + j)),
                  kspec, kspec],
        out_specs=[out, out],
        out_shape=[jax.ShapeDtypeStruct((n, NB * D), BF16), jax.ShapeDtypeStruct((n, NB * D), BF16)],
        compiler_params=_cp("parallel", "arbitrary"),
        name="hyena_fwd",
    )(cmat, smat, x2d, kre, kim)


def _dfti_kernel(c_ref, st_ref, qa_ref, qb_ref, xk_ref, z_ref, bias_ref, *o_refs):
    y = _bdot(c_ref[...], qa_ref[...]) + _bdot(st_ref[...], qb_ref[...])
    o = xk_ref[...].astype(F32) * (y + bias_ref[0] * z_ref[...].astype(F32))
    for o_ref in o_refs:
        o_ref[...] = o.astype(o_ref.dtype)


def hyena_inv(tabs, qa, qb, u2d, xpart, z2d, zpart, bias, order, out_dtypes):
    cmat, _, smat_t = tabs
    n = cmat.shape[0]
    tf = min(1024, n)
    tn = 512
    nb = D // tn
    out = pl.BlockSpec((tf, tn), lambda i, j: (i, j))
    q = pl.BlockSpec((n, tn), lambda i, j: (0, j))
    return pl.pallas_call(
        _dfti_kernel,
        grid=(n // tf, NB * nb),
        in_specs=[_table_spec(tf, n), _table_spec(tf, n), q, q,
                  pl.BlockSpec((tf, tn), lambda i, j: (i, xpart * NB * nb + j)),
                  pl.BlockSpec((tf, tn), lambda i, j: (i, zpart * NB * nb + j)),
                  pl.BlockSpec((1, 1, tn), lambda i, j: (order, 0, j % nb))],
        out_specs=[out] * len(out_dtypes),
        out_shape=[jax.ShapeDtypeStruct((n, NB * D), dt) for dt in out_dtypes],
        compiler_params=_cp("parallel", "arbitrary"),
        name="hyena_inv",
    )(cmat, smat_t, qa, qb, u2d, z2d, bias)


DFT_R = 64


def _dft_table_kernel(ac_ref, as_ref, bc_ref, bs_ref, c_ref, s_ref, st_ref):
    f1 = pl.program_id(0)
    ac, as_ = ac_ref[0], as_ref[0]
    bc, bs = bc_ref[...], bs_ref[...]
    c = ac * bc - as_ * bs
    s = as_ * bc + ac * bs
    row = lax.broadcasted_iota(jnp.int32, c.shape, 0)
    col = lax.broadcasted_iota(jnp.int32, c.shape, 1)
    alt_col = jnp.where(col % 2 == 0, 1.0, -1.0)
    alt_row = jnp.where(row % 2 == 0, 1.0, -1.0)
    c_ref[...] = c.astype(c_ref.dtype)
    s_ref[...] = jnp.where((row == 0) & (f1 == 0), alt_col, s).astype(s_ref.dtype)
    st_ref[...] = jnp.where(col == 0, alt_row, s).astype(st_ref.dtype)


def dft_tables(n):
    r = min(DFT_R, n)
    t = np.arange(n, dtype=np.int64)[None, :]
    f1 = np.arange(n // r, dtype=np.int64)[:, None]
    f0 = np.arange(r, dtype=np.int64)[:, None]
    ang_a = ((r * f1 * t) % (2 * n)).astype(np.float64) * (np.pi / n)
    ang_b = ((f0 * t) % (2 * n)).astype(np.float64) * (np.pi / n)
    ac = jnp.asarray(np.cos(ang_a), F32).reshape(n // r, 1, n)
    as_ = jnp.asarray(np.sin(ang_a), F32).reshape(n // r, 1, n)
    bc = jnp.asarray(np.cos(ang_b), F32)
    bs = jnp.asarray(np.sin(ang_b), F32)
    arow = pl.BlockSpec((1, 1, n), lambda i: (i, 0, 0))
    bfull = pl.BlockSpec((r, n), lambda i: (0, 0))
    out = pl.BlockSpec((r, n), lambda i: (i, 0))
    return tuple(pl.pallas_call(
        _dft_table_kernel,
        grid=(n // r,),
        in_specs=[arow, arow, bfull, bfull],
        out_specs=[out, out, out],
        out_shape=[jax.ShapeDtypeStruct((n, n), BF16)] * 3,
        compiler_params=_cp("parallel"),
        name="dft_tables",
    )(ac, as_, bc, bs))


def hyena_embedding(n):
    t = np.linspace(0.0, 1.0, n)[:, None]
    bands = np.linspace(1e-4, HY_BANDS - 1, HY_BANDS)
    ang = (2.0 * math.pi / n) * np.arange(n, dtype=np.float64)[:, None] * bands[None]
    emb = np.concatenate([t, np.cos(ang), np.sin(ang)], axis=-1)
    return jnp.asarray(np.pad(emb, ((0, 0), (0, HY_PAD - emb.shape[1]))), F32)


def hyena_mixer(proj, n, tabs, emb, p):
    u2 = hyena_short(proj, n, p["short_w"], p["short_b"])
    kf, kb = hyena_filters(emb, p)
    kre, kim = hyena_kspec(tabs, kf, kb)
    qa, qb = hyena_fwd(tabs, u2, 0, kre, kim, 0)
    (z2,) = hyena_inv(tabs, qa, qb, u2, 1, u2, 0, p["bias"], 0, (BF16,))
    qa, qb = hyena_fwd(tabs, z2, 0, kre, kim, 1)
    (y,) = hyena_inv(tabs, qa, qb, u2, 2, z2, 0, p["bias"], 1, (BF16,))
    return y


def _softplus(x):
    return jnp.maximum(x, 0.0) + jnp.log(1.0 + jnp.exp(-jnp.abs(x)))


def _split3(x):
    h = x.astype(BF16)
    r = x - h.astype(F32)
    m = r.astype(BF16)
    return h, m, (r - m.astype(F32)).astype(BF16)


def _dot_split(a, b, split_lhs):
    if split_lhs:
        q = a.shape[0]
        r = _bdot(jnp.concatenate(_split3(a), axis=0), b)
        return r[:q] + r[q:2 * q] + r[2 * q:]
    n = b.shape[1]
    r = _bdot(a, jnp.concatenate(_split3(b), axis=1))
    return r[:, :n] + r[:, n:2 * n] + r[:, 2 * n:]


def _ssd_kernel(*refs, rev, nc):
    if rev:
        (xm, dt_ref, yprev, dtb, arow, xexp, h0, y_ref, hlast, s_s) = refs
    else:
        (xm, dt_ref, dtb, arow, dsk, xexp, h0, y_ref, hlast, s_s) = refs
    q = SSM_Q
    i = pl.program_id(1)

    @pl.when(i == 0)
    def _():
        s_s[...] = h0[0]

    xbc = xm[0]
    xs = xbc[:, :D]
    dtp = _softplus(dt_ref[0] + dtb[...])
    dta = dtp * arow[...]
    row = lax.broadcasted_iota(jnp.int32, (q, q), 0)
    col = lax.broadcasted_iota(jnp.int32, (q, q), 1)
    mask = (row <= col) if rev else (row >= col)
    cum = _dot_split(mask.astype(BF16), dta, split_lhs=False)
    cum_t = cum.T
    xe = xexp[...]
    cum_e = _dot_split(cum, xe, split_lhs=True)
    dt_e = _dot_split(dtp, xe, split_lhs=True)
    last = 0 if rev else q - 1
    cum_last = cum_e[last:last + 1, :]
    xdt = xs * dt_e
    xdt_b = xdt.astype(BF16)
    s_old = s_s[...]
    s_old_b = s_old.astype(BF16)
    w_b = (jnp.exp(cum_last - cum_e) * xdt).astype(BF16)
    lane = lax.broadcasted_iota(jnp.int32, (q, 2 * SSM_P), 1)
    gw = D // SSM_G
    e_per_g = SSM_H // SSM_G
    slot0 = SSM_H if rev else 0
    y_cols = []
    s_cols = []
    for g in range(SSM_G):
        bg = xbc[:, D + g * SSM_N:D + (g + 1) * SSM_N].astype(BF16)
        cg = xbc[:, D + SSM_G * SSM_N + g * SSM_N:D + SSM_G * SSM_N + (g + 1) * SSM_N].astype(BF16)
        cbm = lax.dot_general(cg, bg, (((1,), (1,)), ((), ())), preferred_element_type=F32)
        for pr in range(e_per_g // 2):
            c0 = g * gw + pr * 2 * SSM_P
            xpair = xdt[:, c0:c0 + 2 * SSM_P]
            mhs, xhs = [], []
            for hh in range(2):
                slot = slot0 + g * e_per_g + pr * 2 + hh
                seg = cum[:, slot:slot + 1] - cum_t[slot:slot + 1, :]
                dec = jnp.exp(jnp.where(mask, seg, -jnp.inf))
                mhs.append((cbm * dec).astype(BF16))
                in_head = (lane >= hh * SSM_P) & (lane < (hh + 1) * SSM_P)
                xhs.append(jnp.where(in_head, xpair, 0.0).astype(BF16))
            y_cols.append(_bdot(jnp.concatenate(mhs, axis=1), jnp.concatenate(xhs, axis=0)))
        gs = slice(g * gw, (g + 1) * gw)
        y_state = _bdot(cg, s_old_b[:, gs])
        y_cols[-(e_per_g // 2):] = [
            y_cols[-(e_per_g // 2) + pr] + jnp.exp(cum_e[:, g * gw + pr * 2 * SSM_P:g * gw + (pr + 1) * 2 * SSM_P])
            * y_state[:, pr * 2 * SSM_P:(pr + 1) * 2 * SSM_P] for pr in range(e_per_g // 2)]
        s_new = jnp.exp(cum_last[:, gs]) * s_old[:, gs] + lax.dot_general(
            bg, w_b[:, gs], (((0,), (0,)), ((), ())), preferred_element_type=F32)
        s_cols.append(s_new)
    y = jnp.concatenate(y_cols, axis=1)
    s_new = jnp.concatenate(s_cols, axis=1)
    s_s[...] = s_new
    hlast[0] = s_new
    if rev:
        y_ref[0] = (y + yprev[0]).astype(y_ref.dtype)
    else:
        y_ref[0] = y + dsk[...] * xs


def _scan_perm(n, col_major, inverse):
    r_ = min(n // GRID_W, TT) if col_major else TT
    s_ = TT // r_
    src = np.arange(TT * NB)
    b, rs = src % NB, src // NB
    r, s = rs // s_, rs % s_
    p = np.zeros((TT * NB, TT * NB), np.float32)
    p[b * TT + s * r_ + r, src] = 1.0
    return jnp.asarray(p.T if inverse else p, BF16)


def _scan_in_kernel(x_ref, xp_ref, xn_ref, cw_ref, cb_ref, perm_ref, o_ref, *, n_k):
    k = pl.program_id(0)
    r = TT * NB
    c = x_ref.shape[-1]
    x = x_ref[...].reshape(r, c)
    prev = jnp.where(k == 0, 0.0, xp_ref[...].reshape(NB * NB, c)[NB * NB - 2 * NB:, :D_XBC])
    nxt = jnp.where(k == n_k - 1, 0.0, xn_ref[...].reshape(NB * NB, c)[:NB, :D_XBC])
    ext = jnp.concatenate([prev, x[:, :D_XBC], nxt], axis=0)
    acc = cb_ref[...]
    for j in range(4):
        acc = acc + cw_ref[j:j + 1, :] * ext[NB * j:NB * j + r]
    x = jnp.concatenate([acc * jax.nn.sigmoid(acc), x[:, D_XBC:]], axis=1)
    hi = x.astype(BF16)
    lo = (x - hi.astype(F32)).astype(BF16)
    p = perm_ref[...]
    y = _bdot(p, hi) + _bdot(p, lo)
    for bb in range(NB):
        o_ref[bb] = y[bb * TT:(bb + 1) * TT]


def _tile_spec(n, col_major, width, cblk):
    if col_major:
        rows = n // GRID_W
        r_ = min(rows, TT)
        assert r_ == rows, "a scan tile must cover whole grid columns"
        return (rows, GRID_W, NB), pl.BlockSpec((rows, TT // r_, NB, width), lambda k: (0, k, 0, cblk))
    return (n // TT, TT, NB), pl.BlockSpec((1, TT, NB, width), lambda k: (k, 0, 0, cblk))


def scan_in(proj, n, col_major, conv_w, conv_b):
    lead, spec = _tile_spec(n, col_major, D_SSD, OFF_XBC // D_SSD)
    cblk = OFF_XBC // D_SSD
    n_k = n // TT
    last8 = TT // NB - 1
    if col_major:
        assert n // GRID_W == TT
        prev = pl.BlockSpec((NB, 1, NB, D_SSD), lambda k: (last8, jnp.maximum(k - 1, 0), 0, cblk))
        nxt = pl.BlockSpec((NB, 1, NB, D_SSD), lambda k: (0, jnp.minimum(k + 1, n_k - 1), 0, cblk))
    else:
        prev = pl.BlockSpec((1, NB, NB, D_SSD), lambda k: (jnp.maximum(k - 1, 0), last8, 0, cblk))
        nxt = pl.BlockSpec((1, NB, NB, D_SSD), lambda k: (jnp.minimum(k + 1, n_k - 1), 0, 0, cblk))
    p4 = proj.reshape(lead + (proj.shape[-1],))
    return pl.pallas_call(
        functools.partial(_scan_in_kernel, n_k=n_k),
        grid=(n_k,),
        in_specs=[spec, prev, nxt, pl.BlockSpec((4, D_XBC), lambda k: (0, 0)), pl.BlockSpec((1, D_XBC), lambda k: (0, 0)),
                  pl.BlockSpec((TT * NB, TT * NB), lambda k: (0, 0))],
        out_specs=pl.BlockSpec((NB, TT, D_SSD), lambda k: (0, k, 0)),
        out_shape=jax.ShapeDtypeStruct((NB, n, D_SSD), F32),
        compiler_params=_cp("parallel"),
        name="scan_in",
    )(p4, p4, p4, conv_w, conv_b, _scan_perm(n, col_major, False))


def _scan_out_kernel(y_ref, perm_ref, o_ref):
    y = jnp.concatenate([y_ref[bb] for bb in range(NB)], axis=0)
    o_ref[...] = _bdot(perm_ref[...], y).reshape(o_ref.shape)


def scan_out(y, n, col_major):
    lead, spec = _tile_spec(n, col_major, D, 0)
    out = pl.pallas_call(
        _scan_out_kernel,
        grid=(n // TT,),
        in_specs=[pl.BlockSpec((NB, TT, D), lambda k: (0, k, 0)), pl.BlockSpec((TT * NB, TT * NB), lambda k: (0, 0))],
        out_specs=spec,
        out_shape=jax.ShapeDtypeStruct(lead + (D,), F32),
        compiler_params=_cp("parallel"),
        name="scan_out",
    )(y, _scan_perm(n, col_major, True))
    return out.reshape(n * NB, D)


def ssd_pass(xs, rev, h0, p, yprev=None):
    _, n, _ = xs.shape
    q = SSM_Q
    nc = n // q
    d_idx = 1 if rev else 0

    def cix(i):
        return (nc - 1 - i) if rev else i

    in_specs = [pl.BlockSpec((1, q, D_XBC), lambda b, i: (b, cix(i), 0)),
                pl.BlockSpec((1, q, 128), lambda b, i: (b, cix(i), D_XBC // 128))]
    args = [xs, xs]
    ymain = pl.BlockSpec((1, q, D), lambda b, i: (b, cix(i), 0))
    if rev:
        in_specs.append(ymain)
        args.append(yprev)
    full = lambda s: pl.BlockSpec(s, lambda b, i: (0,) * len(s))
    in_specs += [full((1, 128)), full((1, 128))]
    args += [p["dt_bias"], p["arow"][d_idx]]
    if not rev:
        in_specs.append(full((1, D)))
        args.append(p["d_skip"])
    in_specs += [full((128, D)), pl.BlockSpec((1, SSM_N, D), lambda b, i: (b, 0, 0))]
    args += [p["xexp"][d_idx], h0]
    y, hlast = pl.pallas_call(
        functools.partial(_ssd_kernel, rev=rev, nc=nc),
        grid=(NB, nc),
        in_specs=in_specs,
        out_specs=[ymain, pl.BlockSpec((1, SSM_N, D), lambda b, i: (b, 0, 0))],
        out_shape=[jax.ShapeDtypeStruct((NB, n, D), BF16 if rev else F32), jax.ShapeDtypeStruct((NB, SSM_N, D), F32)],
        scratch_shapes=[pltpu.VMEM((SSM_N, D), F32)],
        compiler_params=_cp("parallel", "arbitrary"),
        name="ssd_bwd" if rev else "ssd_fwd",
    )(*args)
    return y, hlast


def ssd_bidir(xs, hf0, hb0, p):
    yf, sf = ssd_pass(xs, False, hf0, p)
    y, sb = ssd_pass(xs, True, hb0, p, yprev=yf)
    return y, sf, sb


def _merge_kernel(x_ref, gr_ref, gh_ref, gs_ref, yr_ref, yh_ref, ys_ref, z_ref, nw_ref, wb_ref, wo_ref, g1_ref,
                  perm_ref, o_ref):
    z = z_ref[...]
    ys = _rms(ys_ref[...].astype(F32) * (z * jax.nn.sigmoid(z)), nw_ref[...])
    m = jax.nn.sigmoid(gr_ref[...]) * _bdot(yr_ref[...], wb_ref[0])
    yh = jnp.concatenate([yh_ref[:, bb * D:(bb + 1) * D] for bb in range(NB)], axis=0)
    yh = _bdot(perm_ref[...], yh).astype(BF16)
    m = m + jax.nn.sigmoid(gh_ref[...]) * _bdot(yh, wb_ref[1])
    m = m + jax.nn.sigmoid(gs_ref[...]) * _bdot(ys.astype(BF16), wb_ref[2])
    o = _bdot(m.astype(BF16), wo_ref[...])
    o_ref[...] = x_ref[...] + _mod_rows(o, g1_ref[...], None)


def merge(x, proj, yr, yh, ys, norm_w, wb, wo, mod, mrow):
    m = x.shape[0]
    row = lambda w, c: pl.BlockSpec((TM, w), lambda i: (i, c))
    return pl.pallas_call(
        _merge_kernel,
        grid=(m // TM,),
        in_specs=[row(D, 0), row(D, OFF_GT // D), row(D, OFF_GT // D + 1), row(D, OFF_GT // D + 2), row(D, 0),
                  pl.BlockSpec((TM // NB, NB * D), lambda i: (i, 0)), row(D, 0), row(D, OFF_SZ // D),
                  pl.BlockSpec((1, D), lambda i: (0, 0)),
                  pl.BlockSpec((3, D, D), lambda i: (0, 0, 0)),
                  pl.BlockSpec((D, D), lambda i: (0, 0)),
                  pl.BlockSpec((NB, D), lambda i: (mrow, 2)),
                  pl.BlockSpec((TM, TM), lambda i: (0, 0))],
        out_specs=row(D, 0),
        out_shape=jax.ShapeDtypeStruct((m, D), F32),
        compiler_params=_cp("parallel"),
        name="merge",
    )(x, proj, proj, proj, yr, yh, ys, proj, norm_w, wb, wo, mod, _row_perm(False))


def _ffn_kernel(x_ref, g_ref, sh_ref, sc_ref, g2_ref, wg_ref, wu_ref, wd_ref, fn_ref, o_ref, xn_s, acc_s, *, nf, final):
    f = pl.program_id(1)

    @pl.when(f == 0)
    def _():
        y = _rms(x_ref[...], g_ref[...])
        xn_s[...] = _mod_rows(y, 1.0 + sc_ref[...], sh_ref[...]).astype(BF16)
        acc_s[...] = jnp.zeros_like(acc_s)

    xn = xn_s[...]
    hg = _bdot(xn, wg_ref[...])
    hu = _bdot(xn, wu_ref[...])
    hh = (hg * jax.nn.sigmoid(hg) * hu).astype(BF16)
    acc_s[...] += _bdot(hh, wd_ref[...])

    @pl.when(f == nf - 1)
    def _():
        o = x_ref[...] + _mod_rows(acc_s[...], g2_ref[...], None)
        if final:
            o = _rms(o, fn_ref[...])
        o_ref[...] = o


def ffn(x, g, mod, mrow, w_up, w_down, final_norm, final):
    m = x.shape[0]
    d_ff = w_down.shape[0]
    nf = 2
    tf = d_ff // nf
    vec = pl.BlockSpec((1, D), lambda i, f: (0, 0))
    return pl.pallas_call(
        functools.partial(_ffn_kernel, nf=nf, final=final),
        grid=(m // TM, nf),
        in_specs=[pl.BlockSpec((TM, D), lambda i, f: (i, 0)), vec,
                  pl.BlockSpec((NB, D), lambda i, f: (mrow, 3)),
                  pl.BlockSpec((NB, D), lambda i, f: (mrow, 4)),
                  pl.BlockSpec((NB, D), lambda i, f: (mrow, 5)),
                  pl.BlockSpec((D, tf), lambda i, f: (0, f)),
                  pl.BlockSpec((D, tf), lambda i, f: (0, nf + f)),
                  pl.BlockSpec((tf, D), lambda i, f: (f, 0)), vec],
        out_specs=pl.BlockSpec((TM, D), lambda i, f: (i, 0)),
        out_shape=jax.ShapeDtypeStruct((m, D), F32),
        scratch_shapes=[pltpu.VMEM((TM, D), BF16), pltpu.VMEM((TM, D), F32)],
        compiler_params=_cp("parallel", "arbitrary"),
        name="ffn",
    )(x, g, mod, mod, mod, w_up, w_up, w_down, final_norm)


def _pack_w_in(w):
    o_sp = 6 * D
    o_gt = o_sp + D_XBC + 2 * SSM_H
    dt = jnp.pad(w[:, o_sp + D_XBC:o_gt], ((0, 0), (0, OFF_GT - OFF_DT - 2 * SSM_H)))
    out = jnp.concatenate([w[:, :o_sp], w[:, o_sp:o_sp + D_XBC], dt, w[:, o_gt:o_gt + 3 * D]], axis=1)
    assert out.shape[1] == D_INP
    return out.astype(BF16)


def _pad_to(a, shape):
    return jnp.pad(a, [(0, s - d) for s, d in zip(shape, a.shape)])


def kernel(x, c, ctx, c_ctx, w_mod, b_mod, norm_mix, norm_ffn, w_in, rnn_conv_w, rnn_conv_b, rnn_gate_a_w, rnn_gate_a_b, rnn_gate_x_w, rnn_gate_x_b, rnn_lambda, hy_short_w, hy_short_b, hy_w1, hy_b1, hy_w2, hy_b2, hy_w3, hy_freq, hy_decay, hy_bias, ssm_conv_w, ssm_conv_b, ssm_a_log, ssm_dt_bias, ssm_d, ssm_norm, w_branch, w_out, w_up, w_down, final_norm):
    bsz, n_l, d = x.shape
    n_c = ctx.shape[1]
    depth = w_in.shape[0]
    assert bsz == NB and d == D

    xl = x.transpose(1, 0, 2).reshape(n_l * NB, D)
    sc = ctx.transpose(1, 0, 2).reshape(n_c * NB, D)
    c16 = jnp.concatenate([c, jnp.broadcast_to(c_ctx[None], (NB, D))], axis=0)
    mods = modulation(c16, w_mod, b_mod)

    tabs = {n_l: dft_tables(n_l), n_c: dft_tables(n_c)}
    embs = {n_l: hyena_embedding(n_l), n_c: hyena_embedding(n_c)}
    head_lane = jnp.arange(128)[:, None]
    hp_head = (jnp.arange(D) // SSM_P)[None, :]
    xexp = jnp.stack([(head_lane == hp_head + SSM_H * dd).astype(BF16) for dd in range(2)])
    zeros_h = jnp.zeros((NB, D), F32)
    zeros_s = jnp.zeros((NB, SSM_N, D), F32)
    fn = final_norm.reshape(1, D)

    for i in range(depth):
        ctx_out = i < depth - 1
        mod = mods[i]
        w_i = _pack_w_in(w_in[i])
        rp = dict(conv_w=rnn_conv_w[i], conv_b=rnn_conv_b[i].reshape(1, D),
                  wa=rnn_gate_a_w[i].astype(BF16), wx=rnn_gate_x_w[i].astype(BF16),
                  ba=rnn_gate_a_b[i].reshape(2, 1, D), bx=rnn_gate_x_b[i].reshape(2, 1, D),
                  lam=rnn_lambda[i].reshape(2, 1, D))
        hp = dict(short_w=hy_short_w[i], short_b=hy_short_b[i].reshape(1, 3 * D),
                  w1=_pad_to(hy_w1[i], (HY_PAD, HY_PAD)), b1=_pad_to(hy_b1[i].reshape(1, -1), (1, HY_PAD)),
                  w2=_pad_to(hy_w2[i], (HY_PAD, HY_PAD)), b2=_pad_to(hy_b2[i].reshape(1, -1), (1, HY_PAD)),
                  freq=_pad_to(hy_freq[i].reshape(1, -1), (1, HY_PAD)),
                  w3=_pad_to(hy_w3[i], (HY_PAD, 4 * D)), decay=hy_decay[i].reshape(1, 4 * D),
                  bias=hy_bias[i].reshape(2, 1, D))
        a_neg = -jnp.exp(ssm_a_log[i].astype(F32))
        arow = jnp.stack([_pad_to(jnp.concatenate([jnp.zeros((SSM_H * dd,), F32), a_neg[dd]])[None], (1, 128))
                          for dd in range(2)])
        sp = dict(conv_w=ssm_conv_w[i], conv_b=ssm_conv_b[i].reshape(1, D_XBC),
                  dt_bias=_pad_to(ssm_dt_bias[i].reshape(1, 2 * SSM_H), (1, 128)), arow=arow,
                  d_skip=jnp.repeat(ssm_d[i], SSM_P).reshape(1, D), xexp=xexp)
        wb = w_branch[i].astype(BF16)
        wo = w_out[i].astype(BF16)
        wu = w_up[i].astype(BF16)
        wd = w_down[i].astype(BF16)
        g_mix = norm_mix[i].reshape(1, D)
        g_ffn = norm_ffn[i].reshape(1, D)
        nrm_s = ssm_norm[i].reshape(1, D)

        proj_c = in_proj(sc, g_mix, mod, 1, w_i)
        proj_l = in_proj(xl, g_mix, mod, 0, w_i)

        hf_c, hf_last = rglru_pass(proj_c, n_c, 0, False, zeros_h, rp)
        yr_c, hb_last = rglru_pass(proj_c, n_c, 1, True, zeros_h, rp, hf=hf_c)
        hf_l, _ = rglru_pass(proj_l, n_l, 0, False, hf_last, rp)
        yr_l, _ = rglru_pass(proj_l, n_l, 1, True, hb_last, rp, hf=hf_l)

        ys_c, sf, sb = ssd_bidir(scan_in(proj_c, n_c, False, sp["conv_w"], sp["conv_b"]), zeros_s, zeros_s, sp)
        ys_l, _, _ = ssd_bidir(scan_in(proj_l, n_l, True, sp["conv_w"], sp["conv_b"]), sf, sb, sp)
        ys_l = scan_out(ys_l, n_l, True)

        yh_l = hyena_mixer(proj_l, n_l, tabs[n_l], embs[n_l], hp)
        xl = merge(xl, proj_l, yr_l, yh_l, ys_l, nrm_s, wb, wo, mod, 0)
        xl = ffn(xl, g_ffn, mod, 0, wu, wd, fn, final=not ctx_out)

        if ctx_out:
            ys_c = scan_out(ys_c, n_c, False)
            yh_c = hyena_mixer(proj_c, n_c, tabs[n_c], embs[n_c], hp)
            sc = merge(sc, proj_c, yr_c, yh_c, ys_c, nrm_s, wb, wo, mod, 1)
            sc = ffn(sc, g_ffn, mod, 1, wu, wd, fn, final=False)

    return xl.reshape(n_l, NB, D).transpose(1, 0, 2)
```

```python
import functools
import math

import numpy as np
import jax
import jax.numpy as jnp
from jax import lax
from jax.experimental import pallas as pl
from jax.experimental.pallas import tpu as pltpu

F32 = jnp.float32
BF16 = jnp.bfloat16
HI = lax.Precision.HIGHEST

NB = 8
GRID_W = 64
NORM_EPS = 1e-6
LRU_C = 8.0
LRU_BW = 128
SSM_P = 64
SSM_H = 16
SSM_G = 2
SSM_N = 128
SSM_Q = 128
HY_BANDS = 16
HY_PAD = 128

D = 1024
OFF_RX, OFF_RG, OFF_HY, OFF_SZ, OFF_XBC, OFF_DT, OFF_GT = 0, 1024, 2048, 5120, 6144, 7680, 8192
D_SSD = 2048
D_XBC = 1536
D_INP = 11264

TM = 512


VMEM_LIMIT = 56 * 1024 * 1024


def _cp(*sem):
    return pltpu.CompilerParams(dimension_semantics=sem, vmem_limit_bytes=VMEM_LIMIT)


def _bdot(a, b):
    return jnp.dot(a, b, preferred_element_type=F32)


def _mod_rows(y, scale8, shift8):
    tm, d = y.shape
    y3 = y.reshape(tm // NB, NB, d)
    if scale8 is not None:
        y3 = y3 * scale8[None]
    if shift8 is not None:
        y3 = y3 + shift8[None]
    return y3.reshape(tm, d)


def _rms(x, g):
    ms = jnp.mean(x * x, axis=-1, keepdims=True)
    return x * lax.rsqrt(ms + NORM_EPS) * g


def _mod_kernel(c_ref, w_ref, b_ref, o_ref):
    c = c_ref[...]
    s = c * jax.nn.sigmoid(c)
    o_ref[0] = jnp.dot(s, w_ref[0], precision=HI, preferred_element_type=F32) + b_ref[0]


def modulation(c16, w_mod, b_mod):
    nl, d, n = w_mod.shape
    tn = 1536
    return pl.pallas_call(
        _mod_kernel,
        grid=(nl, n // tn),
        in_specs=[pl.BlockSpec((16, d), lambda l, j: (0, 0)),
                  pl.BlockSpec((1, d, tn), lambda l, j: (l, 0, j)),
                  pl.BlockSpec((1, 1, tn), lambda l, j: (l, 0, j))],
        out_specs=pl.BlockSpec((1, 16, tn), lambda l, j: (l, 0, j)),
        out_shape=jax.ShapeDtypeStruct((nl, 16, n), F32),
        compiler_params=_cp("parallel", "parallel"),
        name="modulation",
    )(c16, w_mod, b_mod.reshape(nl, 1, n))


def _inproj_kernel(x_ref, g_ref, sh_ref, sc_ref, w_ref, o_ref, xn_ref):
    @pl.when(pl.program_id(1) == 0)
    def _():
        y = _rms(x_ref[...], g_ref[...])
        xn_ref[...] = _mod_rows(y, 1.0 + sc_ref[...], sh_ref[...]).astype(BF16)

    o_ref[...] = _bdot(xn_ref[...], w_ref[...])


def in_proj(x, g, mod, mrow, w):
    m, d = x.shape
    n = w.shape[1]
    tn = 1024
    tm = min(2048, m)
    return pl.pallas_call(
        _inproj_kernel,
        grid=(m // tm, n // tn),
        in_specs=[pl.BlockSpec((tm, d), lambda i, j: (i, 0)),
                  pl.BlockSpec((1, d), lambda i, j: (0, 0)),
                  pl.BlockSpec((NB, d), lambda i, j: (mrow, 0)),
                  pl.BlockSpec((NB, d), lambda i, j: (mrow, 1)),
                  pl.BlockSpec((d, tn), lambda i, j: (0, j))],
        out_specs=pl.BlockSpec((tm, tn), lambda i, j: (i, j)),
        out_shape=jax.ShapeDtypeStruct((m, n), F32),
        scratch_shapes=[pltpu.VMEM((tm, d), BF16)],
        compiler_params=_cp("parallel", "arbitrary"),
        name="in_proj",
    )(x, g, mod, mod, w)


def _expm1_neg(x):
    poly = x * (1.0 + x * (0.5 + x * (1.0 / 6 + x * (1.0 / 24 + x * (1.0 / 120 + x * (1.0 / 720))))))
    return jnp.where(x > -0.1, poly, jnp.exp(x) - 1.0)


def _rglru_kernel(*refs, rev, final, tt, cb, n_t):
    if final:
        (xm, xp, xn, rg, hf, cw, cbias, wa, wx, ba, bx, lam, h0, out, hlast, ext_s, a_s, g_s, h_s) = refs
    else:
        (xm, xp, xn, cw, cbias, wa, wx, ba, bx, lam, h0, out, hlast, ext_s, a_s, g_s, h_s) = refs
    r = tt * NB
    i = pl.program_id(1)
    ti = (n_t - 1 - i) if rev else i

    @pl.when(i == 0)
    def _():
        h_s[...] = h0[...]

    ext_s[0:16, :] = jnp.where(ti == 0, 0.0, xp[...])
    ext_s[16:16 + r, :] = xm[...]
    ext_s[16 + r:24 + r, :] = jnp.where(ti == n_t - 1, 0.0, xn[...])
    lam_v = lam[...]
    sp = jnp.maximum(-lam_v, 0.0) + jnp.log(1.0 + jnp.exp(-jnp.abs(lam_v)))
    sub = 128
    for rc in range(r // sub):
        r0 = rc * sub
        for k in range(cb // LRU_BW):
            ls = slice(k * LRU_BW, (k + 1) * LRU_BW)
            u = cbias[:, ls]
            for j in range(4):
                u = u + cw[j:j + 1, ls] * ext_s[r0 + 8 * j:r0 + 8 * j + sub, ls]
            ub = u.astype(BF16)
            rr = jax.nn.sigmoid(_bdot(ub, wa[k]) + ba[:, ls])
            ii = jax.nn.sigmoid(_bdot(ub, wx[k]) + bx[:, ls])
            la = (-LRU_C) * rr * sp[:, ls]
            a_s[r0:r0 + sub, ls] = jnp.exp(la)
            g_s[r0:r0 + sub, ls] = jnp.sqrt(-_expm1_neg(2.0 * la)) * (ii * u)

    def step(s, h):
        t = (tt - 1 - s) if rev else s
        rows = pl.ds(pl.multiple_of(t * NB, NB), NB)
        h = a_s[rows, :] * h + g_s[rows, :]
        g_s[rows, :] = h
        return h

    h = lax.fori_loop(0, tt, step, h_s[...], unroll=8)
    h_s[...] = h
    hlast[...] = h
    if final:
        for rc in range(r // sub):
            rs = slice(rc * sub, (rc + 1) * sub)
            out[rs, :] = ((hf[rs, :] + g_s[rs, :]) * jax.nn.gelu(rg[rs, :])).astype(out.dtype)
    else:
        out[...] = g_s[...]


def rglru_pass(proj, n, d_idx, rev, h0, p, hf=None):
    final = hf is not None
    tt = min(64, n)
    cb = D
    r = tt * NB
    n_t = n // tt
    nc = D // cb

    def tix(i):
        return (n_t - 1 - i) if rev else i

    main = pl.BlockSpec((r, cb), lambda c, i: (tix(i), c))
    prev = pl.BlockSpec((16, cb), lambda c, i: (jnp.maximum(tix(i) * (r // 16) - 1, 0), c))
    nxt = pl.BlockSpec((NB, cb), lambda c, i: (jnp.minimum((tix(i) + 1) * tt, n - 1), c))
    vec = pl.BlockSpec((1, cb), lambda c, i: (0, c))
    wsp = pl.BlockSpec((cb // LRU_BW, LRU_BW, LRU_BW), lambda c, i: (c, 0, 0))
    in_specs = [main, prev, nxt]
    args = [proj, proj, proj]
    if final:
        in_specs += [pl.BlockSpec((r, cb), lambda c, i: (tix(i), OFF_RG // cb + c)), main]
        args += [proj, hf]
    in_specs += [pl.BlockSpec((4, cb), lambda c, i: (0, c)), vec, wsp, wsp, vec, vec, vec,
                 pl.BlockSpec((NB, cb), lambda c, i: (0, c))]
    args += [p["conv_w"], p["conv_b"], p["wa"][d_idx], p["wx"][d_idx], p["ba"][d_idx], p["bx"][d_idx],
             p["lam"][d_idx], h0]
    out, hlast = pl.pallas_call(
        functools.partial(_rglru_kernel, rev=rev, final=final, tt=tt, cb=cb, n_t=n_t),
        grid=(nc, n_t),
        in_specs=in_specs,
        out_specs=[main, pl.BlockSpec((NB, cb), lambda c, i: (0, c))],
        out_shape=[jax.ShapeDtypeStruct((n * NB, D), BF16 if final else F32),
                   jax.ShapeDtypeStruct((NB, D), F32)],
        scratch_shapes=[pltpu.VMEM((r + 24, cb), F32), pltpu.VMEM((r, cb), F32), pltpu.VMEM((r, cb), F32),
                        pltpu.VMEM((NB, cb), F32)],
        compiler_params=_cp("parallel", "arbitrary"),
        name="rglru_bwd" if rev else "rglru_fwd",
    )(*args)
    return out, hlast


TT = 64


def _row_perm(to_batch_major):
    r = TT * NB
    src = np.arange(r)
    t, b = src // NB, src % NB
    p = np.zeros((r, r), np.float32)
    p[b * TT + t, src] = 1.0
    return jnp.asarray(p if to_batch_major else p.T, BF16)


def _hyshort_kernel(xm, xp, xn, w, b, perm, o, *, n_t):
    i = pl.program_id(0)
    r = xm.shape[0]
    tt = r // NB
    prev = jnp.where(i == 0, 0.0, xp[...])
    nxt = jnp.where(i == n_t - 1, 0.0, xn[...])
    x = xm[...]
    sh_prev = jnp.concatenate([prev, x[:r - NB]], axis=0)
    sh_next = jnp.concatenate([x[NB:], nxt], axis=0)
    u = w[0:1, :] * sh_prev + w[1:2, :] * x + w[2:3, :] * sh_next + b[...]
    ub = _bdot(perm[...], u.astype(BF16))
    for bb in range(NB):
        o[:, bb * D:(bb + 1) * D] = ub[bb * tt:(bb + 1) * tt].astype(o.dtype)


def hyena_short(proj, n, w, b):
    tt = min(TT, n)
    assert tt == TT
    r = tt * NB
    n_t = n // tt
    c0 = OFF_HY // D
    return pl.pallas_call(
        functools.partial(_hyshort_kernel, n_t=n_t),
        grid=(n_t, 3),
        in_specs=[pl.BlockSpec((r, D), lambda i, c: (i, c0 + c)),
                  pl.BlockSpec((NB, D), lambda i, c: (jnp.maximum(i * tt - 1, 0), c0 + c)),
                  pl.BlockSpec((NB, D), lambda i, c: (jnp.minimum((i + 1) * tt, n - 1), c0 + c)),
                  pl.BlockSpec((3, D), lambda i, c: (0, c)),
                  pl.BlockSpec((1, D), lambda i, c: (0, c)),
                  pl.BlockSpec((r, r), lambda i, c: (0, 0))],
        out_specs=pl.BlockSpec((tt, NB * D), lambda i, c: (i, c)),
        out_shape=jax.ShapeDtypeStruct((n, 3 * NB * D), BF16),
        compiler_params=_cp("parallel", "parallel"),
        name="hyena_short",
    )(proj, proj, proj, w, b, _row_perm(True))


def _hyfilt_kernel(emb, w1, b1, w2, b2, fr, w3f, w3b, dcf, dcb, of, ob, h_s):
    n = emb.shape[0]

    @pl.when(pl.program_id(0) == 0)
    def _():
        f = fr[...]
        h1 = jnp.sin(f * (jnp.dot(emb[...], w1[...], precision=HI, preferred_element_type=F32) + b1[...]))
        h_s[...] = jnp.sin(f * (jnp.dot(h1, w2[...], precision=HI, preferred_element_type=F32) + b2[...]))

    h = h_s[...]
    row = lax.broadcasted_iota(jnp.int32, (n, 1), 0)
    t = row.astype(F32) * (1.0 / (n - 1))
    kf = jnp.dot(h, w3f[...], precision=HI, preferred_element_type=F32) * jnp.exp(-t * dcf[...])
    kb = jnp.dot(h, w3b[...], precision=HI, preferred_element_type=F32) * jnp.exp(-t * dcb[...])
    kb = jnp.where(row == 0, 0.0, kb)
    nrm = jnp.sum(jnp.abs(kf), axis=0, keepdims=True) + jnp.sum(jnp.abs(kb), axis=0, keepdims=True)
    inv = 1.0 / nrm
    of[...] = (kf * inv).astype(of.dtype)
    ob[...] = (kb * inv).astype(ob.dtype)


def hyena_filters(emb, p):
    n = emb.shape[0]
    tn = 256
    nj = 2 * D // tn
    full = lambda s: pl.BlockSpec(s, lambda j: (0, 0))
    return pl.pallas_call(
        _hyfilt_kernel,
        grid=(nj,),
        in_specs=[full((n, HY_PAD)), full((HY_PAD, HY_PAD)), full((1, HY_PAD)), full((HY_PAD, HY_PAD)),
                  full((1, HY_PAD)), full((1, HY_PAD)),
                  pl.BlockSpec((HY_PAD, tn), lambda j: (0, j)), pl.BlockSpec((HY_PAD, tn), lambda j: (0, nj + j)),
                  pl.BlockSpec((1, tn), lambda j: (0, j)), pl.BlockSpec((1, tn), lambda j: (0, nj + j))],
        out_specs=[pl.BlockSpec((n, tn), lambda j: (0, j)), pl.BlockSpec((n, tn), lambda j: (0, j))],
        out_shape=[jax.ShapeDtypeStruct((n, 2 * D), BF16), jax.ShapeDtypeStruct((n, 2 * D), BF16)],
        scratch_shapes=[pltpu.VMEM((n, HY_PAD), F32)],
        compiler_params=_cp("arbitrary"),
        name="hyena_filters",
    )(emb, p["w1"], p["b1"], p["w2"], p["b2"], p["freq"], p["w3"], p["w3"], p["decay"], p["decay"])


def _dftp_kernel(c_ref, s_ref, x_ref, a_ref, b_ref):
    x = x_ref[...]
    a_ref[...] = _bdot(c_ref[...], x)
    b_ref[...] = _bdot(s_ref[...], x)


def _kcombine_kernel(af_ref, ab_ref, bf_ref, bb_ref, kre_ref, kim_ref):
    i = pl.program_id(0)
    kre_ref[...] = af_ref[...] + ab_ref[...]
    bf, bb = bf_ref[...], bb_ref[...]
    first = (lax.broadcasted_iota(jnp.int32, bf.shape, 0) == 0) & (i == 0)
    kim_ref[...] = jnp.where(first, bf + bb, bb - bf)


def hyena_kspec(tabs, kf, kb):
    cmat, smat, _ = tabs
    n = cmat.shape[0]
    tf = min(1024, n)
    tn = 512
    x = jnp.concatenate([kf, kb], axis=1)
    nj = x.shape[1] // D
    out = pl.BlockSpec((tf, tn), lambda i, j: (i, j))
    a, b = pl.pallas_call(
        _dftp_kernel,
        grid=(n // tf, x.shape[1] // tn),
        in_specs=[_table_spec(tf, n), _table_spec(tf, n), pl.BlockSpec((n, tn), lambda i, j: (0, j))],
        out_specs=[out, out],
        out_shape=[jax.ShapeDtypeStruct(x.shape, F32), jax.ShapeDtypeStruct(x.shape, F32)],
        compiler_params=_cp("parallel", "arbitrary"),
        name="hyena_kdft",
    )(cmat, smat, x)
    fwd = pl.BlockSpec((tf, D), lambda i, j: (i, j))
    bwd = pl.BlockSpec((tf, D), lambda i, j: (i, nj // 2 + j))
    return pl.pallas_call(
        _kcombine_kernel,
        grid=(n // tf, nj // 2),
        in_specs=[fwd, bwd, fwd, bwd],
        out_specs=[fwd, fwd],
        out_shape=[jax.ShapeDtypeStruct((n, 2 * D), F32), jax.ShapeDtypeStruct((n, 2 * D), F32)],
        compiler_params=_cp("parallel", "parallel"),
        name="hyena_kcombine",
    )(a, a, b, b)


def _dftf_kernel(c_ref, s_ref, x_ref, kre_ref, kim_ref, qa_ref, qb_ref, *, inv_n):
    i = pl.program_id(0)
    x = x_ref[...]
    a = _bdot(c_ref[...], x)
    b = _bdot(s_ref[...], x)
    kre, kim = kre_ref[...], kim_ref[...]
    first = (lax.broadcasted_iota(jnp.int32, a.shape, 0) == 0) & (i == 0)
    qa = jnp.where(first, inv_n * (a * kre), (2.0 * inv_n) * (a * kre + b * kim))
    qb = jnp.where(first, inv_n * (b * kim), (2.0 * inv_n) * (b * kre - a * kim))
    qa_ref[...] = qa.astype(qa_ref.dtype)
    qb_ref[...] = qb.astype(qb_ref.dtype)


def _table_spec(tf, n):
    return pl.BlockSpec((tf, n), lambda i, j: (i, 0), pipeline_mode=pl.Buffered(1))


def hyena_fwd(tabs, x2d, part, kre, kim, order):
    cmat, smat, _ = tabs
    n = cmat.shape[0]
    tf = min(1024, n)
    tn = 512
    nb = D // tn
    out = pl.BlockSpec((tf, tn), lambda i, j: (i, j))
    kspec = pl.BlockSpec((tf, tn), lambda i, j: (i, order * nb + j % nb))
    return pl.pallas_call(
        functools.partial(_dftf_kernel, inv_n=1.0 / (2 * n)),
        grid=(n // tf, NB * nb),
        in_specs=[_table_spec(tf, n), _table_spec(tf, n),
                  pl.BlockSpec((n, tn), lambda i, j: (0, part * NB * nb + j)),
                  kspec, kspec],
        out_specs=[out, out],
        out_shape=[jax.ShapeDtypeStruct((n, NB * D), BF16), jax.ShapeDtypeStruct((n, NB * D), BF16)],
        compiler_params=_cp("parallel", "arbitrary"),
        name="hyena_fwd",
    )(cmat, smat, x2d, kre, kim)


def _dfti_kernel(c_ref, st_ref, qa_ref, qb_ref, xk_ref, z_ref, bias_ref, *o_refs):
    y = _bdot(c_ref[...], qa_ref[...]) + _bdot(st_ref[...], qb_ref[...])
    o = xk_ref[...].astype(F32) * (y + bias_ref[0] * z_ref[...].astype(F32))
    for o_ref in o_refs:
        o_ref[...] = o.astype(o_ref.dtype)


def hyena_inv(tabs, qa, qb, u2d, xpart, z2d, zpart, bias, order, out_dtypes):
    cmat, _, smat_t = tabs
    n = cmat.shape[0]
    tf = min(1024, n)
    tn = 512
    nb = D // tn
    out = pl.BlockSpec((tf, tn), lambda i, j: (i, j))
    q = pl.BlockSpec((n, tn), lambda i, j: (0, j))
    return pl.pallas_call(
        _dfti_kernel,
        grid=(n // tf, NB * nb),
        in_specs=[_table_spec(tf, n), _table_spec(tf, n), q, q,
                  pl.BlockSpec((tf, tn), lambda i, j: (i, xpart * NB * nb + j)),
                  pl.BlockSpec((tf, tn), lambda i, j: (i, zpart * NB * nb + j)),
                  pl.BlockSpec((1, 1, tn), lambda i, j: (order, 0, j % nb))],
        out_specs=[out] * len(out_dtypes),
        out_shape=[jax.ShapeDtypeStruct((n, NB * D), dt) for dt in out_dtypes],
        compiler_params=_cp("parallel", "arbitrary"),
        name="hyena_inv",
    )(cmat, smat_t, qa, qb, u2d, z2d, bias)


DFT_R = 64


def _dft_table_kernel(ac_ref, as_ref, bc_ref, bs_ref, c_ref, s_ref, st_ref):
    f1 = pl.program_id(0)
    ac, as_ = ac_ref[0], as_ref[0]
    bc, bs = bc_ref[...], bs_ref[...]
    c = ac * bc - as_ * bs
    s = as_ * bc + ac * bs
    row = lax.broadcasted_iota(jnp.int32, c.shape, 0)
    col = lax.broadcasted_iota(jnp.int32, c.shape, 1)
    alt_col = jnp.where(col % 2 == 0, 1.0, -1.0)
    alt_row = jnp.where(row % 2 == 0, 1.0, -1.0)
    c_ref[...] = c.astype(c_ref.dtype)
    s_ref[...] = jnp.where((row == 0) & (f1 == 0), alt_col, s).astype(s_ref.dtype)
    st_ref[...] = jnp.where(col == 0, alt_row, s).astype(st_ref.dtype)


def dft_tables(n):
    r = min(DFT_R, n)
    t = np.arange(n, dtype=np.int64)[None, :]
    f1 = np.arange(n // r, dtype=np.int64)[:, None]
    f0 = np.arange(r, dtype=np.int64)[:, None]
    ang_a = ((r * f1 * t) % (2 * n)).astype(np.float64) * (np.pi / n)
    ang_b = ((f0 * t) % (2 * n)).astype(np.float64) * (np.pi / n)
    ac = jnp.asarray(np.cos(ang_a), F32).reshape(n // r, 1, n)
    as_ = jnp.asarray(np.sin(ang_a), F32).reshape(n // r, 1, n)
    bc = jnp.asarray(np.cos(ang_b), F32)
    bs = jnp.asarray(np.sin(ang_b), F32)
    arow = pl.BlockSpec((1, 1, n), lambda i: (i, 0, 0))
    bfull = pl.BlockSpec((r, n), lambda i: (0, 0))
    out = pl.BlockSpec((r, n), lambda i: (i, 0))
    return tuple(pl.pallas_call(
        _dft_table_kernel,
        grid=(n // r,),
        in_specs=[arow, arow, bfull, bfull],
        out_specs=[out, out, out],
        out_shape=[jax.ShapeDtypeStruct((n, n), BF16)] * 3,
        compiler_params=_cp("parallel"),
        name="dft_tables",
    )(ac, as_, bc, bs))


def hyena_embedding(n):
    t = np.linspace(0.0, 1.0, n)[:, None]
    bands = np.linspace(1e-4, HY_BANDS - 1, HY_BANDS)
    ang = (2.0 * math.pi / n) * np.arange(n, dtype=np.float64)[:, None] * bands[None]
    emb = np.concatenate([t, np.cos(ang), np.sin(ang)], axis=-1)
    return jnp.asarray(np.pad(emb, ((0, 0), (0, HY_PAD - emb.shape[1]))), F32)


def hyena_mixer(proj, n, tabs, emb, p):
    u2 = hyena_short(proj, n, p["short_w"], p["short_b"])
    kf, kb = hyena_filters(emb, p)
    kre, kim = hyena_kspec(tabs, kf, kb)
    qa, qb = hyena_fwd(tabs, u2, 0, kre, kim, 0)
    (z2,) = hyena_inv(tabs, qa, qb, u2, 1, u2, 0, p["bias"], 0, (BF16,))
    qa, qb = hyena_fwd(tabs, z2, 0, kre, kim, 1)
    (y,) = hyena_inv(tabs, qa, qb, u2, 2, z2, 0, p["bias"], 1, (BF16,))
    return y


def _softplus(x):
    return jnp.maximum(x, 0.0) + jnp.log(1.0 + jnp.exp(-jnp.abs(x)))


def _split3(x):
    h = x.astype(BF16)
    r = x - h.astype(F32)
    m = r.astype(BF16)
    return h, m, (r - m.astype(F32)).astype(BF16)


def _dot_split(a, b, split_lhs):
    if split_lhs:
        q = a.shape[0]
        r = _bdot(jnp.concatenate(_split3(a), axis=0), b)
        return r[:q] + r[q:2 * q] + r[2 * q:]
    n = b.shape[1]
    r = _bdot(a, jnp.concatenate(_split3(b), axis=1))
    return r[:, :n] + r[:, n:2 * n] + r[:, 2 * n:]


def _ssd_kernel(*refs, rev, nc):
    if rev:
        (xm, dt_ref, yprev, dtb, arow, xexp, h0, y_ref, hlast, s_s) = refs
    else:
        (xm, dt_ref, dtb, arow, dsk, xexp, h0, y_ref, hlast, s_s) = refs
    q = SSM_Q
    i = pl.program_id(1)

    @pl.when(i == 0)
    def _():
        s_s[...] = h0[0]

    xbc = xm[0]
    xs = xbc[:, :D]
    dtp = _softplus(dt_ref[0] + dtb[...])
    dta = dtp * arow[...]
    row = lax.broadcasted_iota(jnp.int32, (q, q), 0)
    col = lax.broadcasted_iota(jnp.int32, (q, q), 1)
    mask = (row <= col) if rev else (row >= col)
    cum = _dot_split(mask.astype(BF16), dta, split_lhs=False)
    cum_t = cum.T
    xe = xexp[...]
    cum_e = _dot_split(cum, xe, split_lhs=True)
    dt_e = _dot_split(dtp, xe, split_lhs=True)
    last = 0 if rev else q - 1
    cum_last = cum_e[last:last + 1, :]
    xdt = xs * dt_e
    xdt_b = xdt.astype(BF16)
    s_old = s_s[...]
    s_old_b = s_old.astype(BF16)
    w_b = (jnp.exp(cum_last - cum_e) * xdt).astype(BF16)
    lane = lax.broadcasted_iota(jnp.int32, (q, 2 * SSM_P), 1)
    gw = D // SSM_G
    e_per_g = SSM_H // SSM_G
    slot0 = SSM_H if rev else 0
    y_cols = []
    s_cols = []
    for g in range(SSM_G):
        bg = xbc[:, D + g * SSM_N:D + (g + 1) * SSM_N].astype(BF16)
        cg = xbc[:, D + SSM_G * SSM_N + g * SSM_N:D + SSM_G * SSM_N + (g + 1) * SSM_N].astype(BF16)
        cbm = lax.dot_general(cg, bg, (((1,), (1,)), ((), ())), preferred_element_type=F32)
        for pr in range(e_per_g // 2):
            c0 = g * gw + pr * 2 * SSM_P
            xpair = xdt[:, c0:c0 + 2 * SSM_P]
            mhs, xhs = [], []
            for hh in range(2):
                slot = slot0 + g * e_per_g + pr * 2 + hh
                seg = cum[:, slot:slot + 1] - cum_t[slot:slot + 1, :]
                dec = jnp.exp(jnp.where(mask, seg, -jnp.inf))
                mhs.append((cbm * dec).astype(BF16))
                in_head = (lane >= hh * SSM_P) & (lane < (hh + 1) * SSM_P)
                xhs.append(jnp.where(in_head, xpair, 0.0).astype(BF16))
            y_cols.append(_bdot(jnp.concatenate(mhs, axis=1), jnp.concatenate(xhs, axis=0)))
        gs = slice(g * gw, (g + 1) * gw)
        y_state = _bdot(cg, s_old_b[:, gs])
        y_cols[-(e_per_g // 2):] = [
            y_cols[-(e_per_g // 2) + pr] + jnp.exp(cum_e[:, g * gw + pr * 2 * SSM_P:g * gw + (pr + 1) * 2 * SSM_P])
            * y_state[:, pr * 2 * SSM_P:(pr + 1) * 2 * SSM_P] for pr in range(e_per_g // 2)]
        s_new = jnp.exp(cum_last[:, gs]) * s_old[:, gs] + lax.dot_general(
            bg, w_b[:, gs], (((0,), (0,)), ((), ())), preferred_element_type=F32)
        s_cols.append(s_new)
    y = jnp.concatenate(y_cols, axis=1)
    s_new = jnp.concatenate(s_cols, axis=1)
    s_s[...] = s_new
    hlast[0] = s_new
    if rev:
        y_ref[0] = (y + yprev[0]).astype(y_ref.dtype)
    else:
        y_ref[0] = y + dsk[...] * xs


def _scan_perm(n, col_major, inverse):
    r_ = min(n // GRID_W, TT) if col_major else TT
    s_ = TT // r_
    src = np.arange(TT * NB)
    b, rs = src % NB, src // NB
    r, s = rs // s_, rs % s_
    p = np.zeros((TT * NB, TT * NB), np.float32)
    p[b * TT + s * r_ + r, src] = 1.0
    return jnp.asarray(p.T if inverse else p, BF16)


def _scan_in_kernel(x_ref, xp_ref, xn_ref, cw_ref, cb_ref, perm_ref, o_ref, *, n_k):
    k = pl.program_id(0)
    r = TT * NB
    c = x_ref.shape[-1]
    x = x_ref[...].reshape(r, c)
    prev = jnp.where(k == 0, 0.0, xp_ref[...].reshape(NB * NB, c)[NB * NB - 2 * NB:, :D_XBC])
    nxt = jnp.where(k == n_k - 1, 0.0, xn_ref[...].reshape(NB * NB, c)[:NB, :D_XBC])
    ext = jnp.concatenate([prev, x[:, :D_XBC], nxt], axis=0)
    acc = cb_ref[...]
    for j in range(4):
        acc = acc + cw_ref[j:j + 1, :] * ext[NB * j:NB * j + r]
    x = jnp.concatenate([acc * jax.nn.sigmoid(acc), x[:, D_XBC:]], axis=1)
    hi = x.astype(BF16)
    lo = (x - hi.astype(F32)).astype(BF16)
    p = perm_ref[...]
    y = _bdot(p, hi) + _bdot(p, lo)
    for bb in range(NB):
        o_ref[bb] = y[bb * TT:(bb + 1) * TT]


def _tile_spec(n, col_major, width, cblk):
    if col_major:
        rows = n // GRID_W
        r_ = min(rows, TT)
        assert r_ == rows, "a scan tile must cover whole grid columns"
        return (rows, GRID_W, NB), pl.BlockSpec((rows, TT // r_, NB, width), lambda k: (0, k, 0, cblk))
    return (n // TT, TT, NB), pl.BlockSpec((1, TT, NB, width), lambda k: (k, 0, 0, cblk))


def scan_in(proj, n, col_major, conv_w, conv_b):
    lead, spec = _tile_spec(n, col_major, D_SSD, OFF_XBC // D_SSD)
    cblk = OFF_XBC // D_SSD
    n_k = n // TT
    last8 = TT // NB - 1
    if col_major:
        assert n // GRID_W == TT
        prev = pl.BlockSpec((NB, 1, NB, D_SSD), lambda k: (last8, jnp.maximum(k - 1, 0), 0, cblk))
        nxt = pl.BlockSpec((NB, 1, NB, D_SSD), lambda k: (0, jnp.minimum(k + 1, n_k - 1), 0, cblk))
    else:
        prev = pl.BlockSpec((1, NB, NB, D_SSD), lambda k: (jnp.maximum(k - 1, 0), last8, 0, cblk))
        nxt = pl.BlockSpec((1, NB, NB, D_SSD), lambda k: (jnp.minimum(k + 1, n_k - 1), 0, 0, cblk))
    p4 = proj.reshape(lead + (proj.shape[-1],))
    return pl.pallas_call(
        functools.partial(_scan_in_kernel, n_k=n_k),
        grid=(n_k,),
        in_specs=[spec, prev, nxt, pl.BlockSpec((4, D_XBC), lambda k: (0, 0)), pl.BlockSpec((1, D_XBC), lambda k: (0, 0)),
                  pl.BlockSpec((TT * NB, TT * NB), lambda k: (0, 0))],
        out_specs=pl.BlockSpec((NB, TT, D_SSD), lambda k: (0, k, 0)),
        out_shape=jax.ShapeDtypeStruct((NB, n, D_SSD), F32),
        compiler_params=_cp("parallel"),
        name="scan_in",
    )(p4, p4, p4, conv_w, conv_b, _scan_perm(n, col_major, False))


def _scan_out_kernel(y_ref, perm_ref, o_ref):
    y = jnp.concatenate([y_ref[bb] for bb in range(NB)], axis=0)
    o_ref[...] = _bdot(perm_ref[...], y).reshape(o_ref.shape)


def scan_out(y, n, col_major):
    lead, spec = _tile_spec(n, col_major, D, 0)
    out = pl.pallas_call(
        _scan_out_kernel,
        grid=(n // TT,),
        in_specs=[pl.BlockSpec((NB, TT, D), lambda k: (0, k, 0)), pl.BlockSpec((TT * NB, TT * NB), lambda k: (0, 0))],
        out_specs=spec,
        out_shape=jax.ShapeDtypeStruct(lead + (D,), F32),
        compiler_params=_cp("parallel"),
        name="scan_out",
    )(y, _scan_perm(n, col_major, True))
    return out.reshape(n * NB, D)


def ssd_pass(xs, rev, h0, p, yprev=None):
    _, n, _ = xs.shape
    q = SSM_Q
    nc = n // q
    d_idx = 1 if rev else 0

    def cix(i):
        return (nc - 1 - i) if rev else i

    in_specs = [pl.BlockSpec((1, q, D_XBC), lambda b, i: (b, cix(i), 0)),
                pl.BlockSpec((1, q, 128), lambda b, i: (b, cix(i), D_XBC // 128))]
    args = [xs, xs]
    ymain = pl.BlockSpec((1, q, D), lambda b, i: (b, cix(i), 0))
    if rev:
        in_specs.append(ymain)
        args.append(yprev)
    full = lambda s: pl.BlockSpec(s, lambda b, i: (0,) * len(s))
    in_specs += [full((1, 128)), full((1, 128))]
    args += [p["dt_bias"], p["arow"][d_idx]]
    if not rev:
        in_specs.append(full((1, D)))
        args.append(p["d_skip"])
    in_specs += [full((128, D)), pl.BlockSpec((1, SSM_N, D), lambda b, i: (b, 0, 0))]
    args += [p["xexp"][d_idx], h0]
    y, hlast = pl.pallas_call(
        functools.partial(_ssd_kernel, rev=rev, nc=nc),
        grid=(NB, nc),
        in_specs=in_specs,
        out_specs=[ymain, pl.BlockSpec((1, SSM_N, D), lambda b, i: (b, 0, 0))],
        out_shape=[jax.ShapeDtypeStruct((NB, n, D), BF16 if rev else F32), jax.ShapeDtypeStruct((NB, SSM_N, D), F32)],
        scratch_shapes=[pltpu.VMEM((SSM_N, D), F32)],
        compiler_params=_cp("parallel", "arbitrary"),
        name="ssd_bwd" if rev else "ssd_fwd",
    )(*args)
    return y, hlast


def ssd_bidir(xs, hf0, hb0, p):
    yf, sf = ssd_pass(xs, False, hf0, p)
    y, sb = ssd_pass(xs, True, hb0, p, yprev=yf)
    return y, sf, sb


def _merge_kernel(x_ref, gr_ref, gh_ref, gs_ref, yr_ref, yh_ref, ys_ref, z_ref, nw_ref, wb_ref, wo_ref, g1_ref,
                  perm_ref, o_ref):
    z = z_ref[...]
    ys = _rms(ys_ref[...].astype(F32) * (z * jax.nn.sigmoid(z)), nw_ref[...])
    m = jax.nn.sigmoid(gr_ref[...]) * _bdot(yr_ref[...], wb_ref[0])
    yh = jnp.concatenate([yh_ref[:, bb * D:(bb + 1) * D] for bb in range(NB)], axis=0)
    yh = _bdot(perm_ref[...], yh).astype(BF16)
    m = m + jax.nn.sigmoid(gh_ref[...]) * _bdot(yh, wb_ref[1])
    m = m + jax.nn.sigmoid(gs_ref[...]) * _bdot(ys.astype(BF16), wb_ref[2])
    o = _bdot(m.astype(BF16), wo_ref[...])
    o_ref[...] = x_ref[...] + _mod_rows(o, g1_ref[...], None)


def merge(x, proj, yr, yh, ys, norm_w, wb, wo, mod, mrow):
    m = x.shape[0]
    row = lambda w, c: pl.BlockSpec((TM, w), lambda i: (i, c))
    return pl.pallas_call(
        _merge_kernel,
        grid=(m // TM,),
        in_specs=[row(D, 0), row(D, OFF_GT // D), row(D, OFF_GT // D + 1), row(D, OFF_GT // D + 2), row(D, 0),
                  pl.BlockSpec((TM // NB, NB * D), lambda i: (i, 0)), row(D, 0), row(D, OFF_SZ // D),
                  pl.BlockSpec((1, D), lambda i: (0, 0)),
                  pl.BlockSpec((3, D, D), lambda i: (0, 0, 0)),
                  pl.BlockSpec((D, D), lambda i: (0, 0)),
                  pl.BlockSpec((NB, D), lambda i: (mrow, 2)),
                  pl.BlockSpec((TM, TM), lambda i: (0, 0))],
        out_specs=row(D, 0),
        out_shape=jax.ShapeDtypeStruct((m, D), F32),
        compiler_params=_cp("parallel"),
        name="merge",
    )(x, proj, proj, proj, yr, yh, ys, proj, norm_w, wb, wo, mod, _row_perm(False))


def _ffn_kernel(x_ref, g_ref, sh_ref, sc_ref, g2_ref, wg_ref, wu_ref, wd_ref, fn_ref, o_ref, xn_s, acc_s, *, nf, final):
    f = pl.program_id(1)

    @pl.when(f == 0)
    def _():
        y = _rms(x_ref[...], g_ref[...])
        xn_s[...] = _mod_rows(y, 1.0 + sc_ref[...], sh_ref[...]).astype(BF16)
        acc_s[...] = jnp.zeros_like(acc_s)

    xn = xn_s[...]
    hg = _bdot(xn, wg_ref[...])
    hu = _bdot(xn, wu_ref[...])
    hh = (hg * jax.nn.sigmoid(hg) * hu).astype(BF16)
    acc_s[...] += _bdot(hh, wd_ref[...])

    @pl.when(f == nf - 1)
    def _():
        o = x_ref[...] + _mod_rows(acc_s[...], g2_ref[...], None)
        if final:
            o = _rms(o, fn_ref[...])
        o_ref[...] = o


def ffn(x, g, mod, mrow, w_up, w_down, final_norm, final):
    m = x.shape[0]
    d_ff = w_down.shape[0]
    nf = 2
    tf = d_ff // nf
    tm = min(1024, m)
    vec = pl.BlockSpec((1, D), lambda i, f: (0, 0))
    return pl.pallas_call(
        functools.partial(_ffn_kernel, nf=nf, final=final),
        grid=(m // tm, nf),
        in_specs=[pl.BlockSpec((tm, D), lambda i, f: (i, 0)), vec,
                  pl.BlockSpec((NB, D), lambda i, f: (mrow, 3)),
                  pl.BlockSpec((NB, D), lambda i, f: (mrow, 4)),
                  pl.BlockSpec((NB, D), lambda i, f: (mrow, 5)),
                  pl.BlockSpec((D, tf), lambda i, f: (0, f)),
                  pl.BlockSpec((D, tf), lambda i, f: (0, nf + f)),
                  pl.BlockSpec((tf, D), lambda i, f: (f, 0)), vec],
        out_specs=pl.BlockSpec((tm, D), lambda i, f: (i, 0)),
        out_shape=jax.ShapeDtypeStruct((m, D), F32),
        scratch_shapes=[pltpu.VMEM((tm, D), BF16), pltpu.VMEM((tm, D), F32)],
        compiler_params=_cp("parallel", "arbitrary"),
        name="ffn",
    )(x, g, mod, mod, mod, w_up, w_up, w_down, final_norm)


def _pack_w_in(w):
    o_sp = 6 * D
    o_gt = o_sp + D_XBC + 2 * SSM_H
    dt = jnp.pad(w[:, o_sp + D_XBC:o_gt], ((0, 0), (0, OFF_GT - OFF_DT - 2 * SSM_H)))
    out = jnp.concatenate([w[:, :o_sp], w[:, o_sp:o_sp + D_XBC], dt, w[:, o_gt:o_gt + 3 * D]], axis=1)
    assert out.shape[1] == D_INP
    return out.astype(BF16)


def _pad_to(a, shape):
    return jnp.pad(a, [(0, s - d) for s, d in zip(shape, a.shape)])


def kernel(x, c, ctx, c_ctx, w_mod, b_mod, norm_mix, norm_ffn, w_in, rnn_conv_w, rnn_conv_b, rnn_gate_a_w, rnn_gate_a_b, rnn_gate_x_w, rnn_gate_x_b, rnn_lambda, hy_short_w, hy_short_b, hy_w1, hy_b1, hy_w2, hy_b2, hy_w3, hy_freq, hy_decay, hy_bias, ssm_conv_w, ssm_conv_b, ssm_a_log, ssm_dt_bias, ssm_d, ssm_norm, w_branch, w_out, w_up, w_down, final_norm):
    bsz, n_l, d = x.shape
    n_c = ctx.shape[1]
    depth = w_in.shape[0]
    assert bsz == NB and d == D

    xl = x.transpose(1, 0, 2).reshape(n_l * NB, D)
    sc = ctx.transpose(1, 0, 2).reshape(n_c * NB, D)
    c16 = jnp.concatenate([c, jnp.broadcast_to(c_ctx[None], (NB, D))], axis=0)
    mods = modulation(c16, w_mod, b_mod)

    tabs = {n_l: dft_tables(n_l), n_c: dft_tables(n_c)}
    embs = {n_l: hyena_embedding(n_l), n_c: hyena_embedding(n_c)}
    head_lane = jnp.arange(128)[:, None]
    hp_head = (jnp.arange(D) // SSM_P)[None, :]
    xexp = jnp.stack([(head_lane == hp_head + SSM_H * dd).astype(BF16) for dd in range(2)])
    zeros_h = jnp.zeros((NB, D), F32)
    zeros_s = jnp.zeros((NB, SSM_N, D), F32)
    fn = final_norm.reshape(1, D)

    for i in range(depth):
        ctx_out = i < depth - 1
        mod = mods[i]
        w_i = _pack_w_in(w_in[i])
        rp = dict(conv_w=rnn_conv_w[i], conv_b=rnn_conv_b[i].reshape(1, D),
                  wa=rnn_gate_a_w[i].astype(BF16), wx=rnn_gate_x_w[i].astype(BF16),
                  ba=rnn_gate_a_b[i].reshape(2, 1, D), bx=rnn_gate_x_b[i].reshape(2, 1, D),
                  lam=rnn_lambda[i].reshape(2, 1, D))
        hp = dict(short_w=hy_short_w[i], short_b=hy_short_b[i].reshape(1, 3 * D),
                  w1=_pad_to(hy_w1[i], (HY_PAD, HY_PAD)), b1=_pad_to(hy_b1[i].reshape(1, -1), (1, HY_PAD)),
                  w2=_pad_to(hy_w2[i], (HY_PAD, HY_PAD)), b2=_pad_to(hy_b2[i].reshape(1, -1), (1, HY_PAD)),
                  freq=_pad_to(hy_freq[i].reshape(1, -1), (1, HY_PAD)),
                  w3=_pad_to(hy_w3[i], (HY_PAD, 4 * D)), decay=hy_decay[i].reshape(1, 4 * D),
                  bias=hy_bias[i].reshape(2, 1, D))
        a_neg = -jnp.exp(ssm_a_log[i].astype(F32))
        arow = jnp.stack([_pad_to(jnp.concatenate([jnp.zeros((SSM_H * dd,), F32), a_neg[dd]])[None], (1, 128))
                          for dd in range(2)])
        sp = dict(conv_w=ssm_conv_w[i], conv_b=ssm_conv_b[i].reshape(1, D_XBC),
                  dt_bias=_pad_to(ssm_dt_bias[i].reshape(1, 2 * SSM_H), (1, 128)), arow=arow,
                  d_skip=jnp.repeat(ssm_d[i], SSM_P).reshape(1, D), xexp=xexp)
        wb = w_branch[i].astype(BF16)
        wo = w_out[i].astype(BF16)
        wu = w_up[i].astype(BF16)
        wd = w_down[i].astype(BF16)
        g_mix = norm_mix[i].reshape(1, D)
        g_ffn = norm_ffn[i].reshape(1, D)
        nrm_s = ssm_norm[i].reshape(1, D)

        proj_c = in_proj(sc, g_mix, mod, 1, w_i)
        proj_l = in_proj(xl, g_mix, mod, 0, w_i)

        hf_c, hf_last = rglru_pass(proj_c, n_c, 0, False, zeros_h, rp)
        yr_c, hb_last = rglru_pass(proj_c, n_c, 1, True, zeros_h, rp, hf=hf_c)
        hf_l, _ = rglru_pass(proj_l, n_l, 0, False, hf_last, rp)
        yr_l, _ = rglru_pass(proj_l, n_l, 1, True, hb_last, rp, hf=hf_l)

        ys_c, sf, sb = ssd_bidir(scan_in(proj_c, n_c, False, sp["conv_w"], sp["conv_b"]), zeros_s, zeros_s, sp)
        ys_l, _, _ = ssd_bidir(scan_in(proj_l, n_l, True, sp["conv_w"], sp["conv_b"]), sf, sb, sp)
        ys_l = scan_out(ys_l, n_l, True)

        yh_l = hyena_mixer(proj_l, n_l, tabs[n_l], embs[n_l], hp)
        xl = merge(xl, proj_l, yr_l, yh_l, ys_l, nrm_s, wb, wo, mod, 0)
        xl = ffn(xl, g_ffn, mod, 0, wu, wd, fn, final=not ctx_out)

        if ctx_out:
            ys_c = scan_out(ys_c, n_c, False)
            yh_c = hyena_mixer(proj_c, n_c, tabs[n_c], embs[n_c], hp)
            sc = merge(sc, proj_c, yr_c, yh_c, ys_c, nrm_s, wb, wo, mod, 1)
            sc = ffn(sc, g_ffn, mod, 1, wu, wd, fn, final=False)

    return xl.reshape(n_l, NB, D).transpose(1, 0, 2)
```

```python
import functools
import math

import numpy as np
import jax
import jax.numpy as jnp
from jax import lax
from jax.experimental import pallas as pl
from jax.experimental.pallas import tpu as pltpu

F32 = jnp.float32
BF16 = jnp.bfloat16
HI = lax.Precision.HIGHEST

NB = 8
GRID_W = 64
NORM_EPS = 1e-6
LRU_C = 8.0
LRU_BW = 128
SSM_P = 64
SSM_H = 16
SSM_G = 2
SSM_N = 128
SSM_Q = 128
HY_BANDS = 16
HY_PAD = 128

D = 1024
OFF_RX, OFF_RG, OFF_HY, OFF_SZ, OFF_XBC, OFF_DT, OFF_GT = 0, 1024, 2048, 5120, 6144, 7680, 8192
D_SSD = 2048
D_XBC = 1536
D_INP = 11264

TM = 512


VMEM_LIMIT = 56 * 1024 * 1024


def _cp(*sem):
    return pltpu.CompilerParams(dimension_semantics=sem, vmem_limit_bytes=VMEM_LIMIT)


def _bdot(a, b):
    return jnp.dot(a, b, preferred_element_type=F32)


def _mod_rows(y, scale8, shift8):
    tm, d = y.shape
    y3 = y.reshape(tm // NB, NB, d)
    if scale8 is not None:
        y3 = y3 * scale8[None]
    if shift8 is not None:
        y3 = y3 + shift8[None]
    return y3.reshape(tm, d)


def _rms(x, g):
    ms = jnp.mean(x * x, axis=-1, keepdims=True)
    return x * lax.rsqrt(ms + NORM_EPS) * g


def _mod_kernel(c_ref, w_ref, b_ref, o_ref):
    c = c_ref[...]
    s = c * jax.nn.sigmoid(c)
    o_ref[0] = jnp.dot(s, w_ref[0], precision=HI, preferred_element_type=F32) + b_ref[0]


def modulation(c16, w_mod, b_mod):
    nl, d, n = w_mod.shape
    tn = 1536
    return pl.pallas_call(
        _mod_kernel,
        grid=(nl, n // tn),
        in_specs=[pl.BlockSpec((16, d), lambda l, j: (0, 0)),
                  pl.BlockSpec((1, d, tn), lambda l, j: (l, 0, j)),
                  pl.BlockSpec((1, 1, tn), lambda l, j: (l, 0, j))],
        out_specs=pl.BlockSpec((1, 16, tn), lambda l, j: (l, 0, j)),
        out_shape=jax.ShapeDtypeStruct((nl, 16, n), F32),
        compiler_params=_cp("parallel", "parallel"),
        name="modulation",
    )(c16, w_mod, b_mod.reshape(nl, 1, n))


def _inproj_kernel(x_ref, g_ref, sh_ref, sc_ref, w_ref, o_ref, xn_ref):
    @pl.when(pl.program_id(1) == 0)
    def _():
        y = _rms(x_ref[...], g_ref[...])
        xn_ref[...] = _mod_rows(y, 1.0 + sc_ref[...], sh_ref[...]).astype(BF16)

    o_ref[...] = _bdot(xn_ref[...], w_ref[...])


def in_proj(x, g, mod, mrow, w):
    m, d = x.shape
    n = w.shape[1]
    tn = 1024
    tm = min(2048, m)
    return pl.pallas_call(
        _inproj_kernel,
        grid=(m // tm, n // tn),
        in_specs=[pl.BlockSpec((tm, d), lambda i, j: (i, 0)),
                  pl.BlockSpec((1, d), lambda i, j: (0, 0)),
                  pl.BlockSpec((NB, d), lambda i, j: (mrow, 0)),
                  pl.BlockSpec((NB, d), lambda i, j: (mrow, 1)),
                  pl.BlockSpec((d, tn), lambda i, j: (0, j))],
        out_specs=pl.BlockSpec((tm, tn), lambda i, j: (i, j)),
        out_shape=jax.ShapeDtypeStruct((m, n), F32),
        scratch_shapes=[pltpu.VMEM((tm, d), BF16)],
        compiler_params=_cp("parallel", "arbitrary"),
        name="in_proj",
    )(x, g, mod, mod, w)


def _expm1_neg(x):
    poly = x * (1.0 + x * (0.5 + x * (1.0 / 6 + x * (1.0 / 24 + x * (1.0 / 120 + x * (1.0 / 720))))))
    return jnp.where(x > -0.1, poly, jnp.exp(x) - 1.0)


def _rglru_kernel(*refs, rev, final, tt, cb, n_t):
    if final:
        (xm, xp, xn, rg, hf, cw, cbias, wa, wx, ba, bx, lam, h0, out, hlast, ext_s, a_s, g_s, h_s) = refs
    else:
        (xm, xp, xn, cw, cbias, wa, wx, ba, bx, lam, h0, out, hlast, ext_s, a_s, g_s, h_s) = refs
    r = tt * NB
    i = pl.program_id(1)
    ti = (n_t - 1 - i) if rev else i

    @pl.when(i == 0)
    def _():
        h_s[...] = h0[...]

    ext_s[0:16, :] = jnp.where(ti == 0, 0.0, xp[...])
    ext_s[16:16 + r, :] = xm[...]
    ext_s[16 + r:24 + r, :] = jnp.where(ti == n_t - 1, 0.0, xn[...])
    lam_v = lam[...]
    sp = jnp.maximum(-lam_v, 0.0) + jnp.log(1.0 + jnp.exp(-jnp.abs(lam_v)))
    sub = 128
    for rc in range(r // sub):
        r0 = rc * sub
        for k in range(cb // LRU_BW):
            ls = slice(k * LRU_BW, (k + 1) * LRU_BW)
            u = cbias[:, ls]
            for j in range(4):
                u = u + cw[j:j + 1, ls] * ext_s[r0 + 8 * j:r0 + 8 * j + sub, ls]
            ub = u.astype(BF16)
            rr = jax.nn.sigmoid(_bdot(ub, wa[k]) + ba[:, ls])
            ii = jax.nn.sigmoid(_bdot(ub, wx[k]) + bx[:, ls])
            la = (-LRU_C) * rr * sp[:, ls]
            a_s[r0:r0 + sub, ls] = jnp.exp(la)
            g_s[r0:r0 + sub, ls] = jnp.sqrt(-_expm1_neg(2.0 * la)) * (ii * u)

    def step(s, h):
        t = (tt - 1 - s) if rev else s
        rows = pl.ds(pl.multiple_of(t * NB, NB), NB)
        h = a_s[rows, :] * h + g_s[rows, :]
        g_s[rows, :] = h
        return h

    h = lax.fori_loop(0, tt, step, h_s[...], unroll=8)
    h_s[...] = h
    hlast[...] = h
    if final:
        for rc in range(r // sub):
            rs = slice(rc * sub, (rc + 1) * sub)
            out[rs, :] = ((hf[rs, :] + g_s[rs, :]) * jax.nn.gelu(rg[rs, :])).astype(out.dtype)
    else:
        out[...] = g_s[...].astype(out.dtype)


def rglru_pass(proj, n, d_idx, rev, h0, p, hf=None):
    final = hf is not None
    tt = min(64, n)
    cb = D
    r = tt * NB
    n_t = n // tt
    nc = D // cb

    def tix(i):
        return (n_t - 1 - i) if rev else i

    main = pl.BlockSpec((r, cb), lambda c, i: (tix(i), c))
    prev = pl.BlockSpec((16, cb), lambda c, i: (jnp.maximum(tix(i) * (r // 16) - 1, 0), c))
    nxt = pl.BlockSpec((NB, cb), lambda c, i: (jnp.minimum((tix(i) + 1) * tt, n - 1), c))
    vec = pl.BlockSpec((1, cb), lambda c, i: (0, c))
    wsp = pl.BlockSpec((cb // LRU_BW, LRU_BW, LRU_BW), lambda c, i: (c, 0, 0))
    in_specs = [main, prev, nxt]
    args = [proj, proj, proj]
    if final:
        in_specs += [pl.BlockSpec((r, cb), lambda c, i: (tix(i), OFF_RG // cb + c)), main]
        args += [proj, hf]
    in_specs += [pl.BlockSpec((4, cb), lambda c, i: (0, c)), vec, wsp, wsp, vec, vec, vec,
                 pl.BlockSpec((NB, cb), lambda c, i: (0, c))]
    args += [p["conv_w"], p["conv_b"], p["wa"][d_idx], p["wx"][d_idx], p["ba"][d_idx], p["bx"][d_idx],
             p["lam"][d_idx], h0]
    out, hlast = pl.pallas_call(
        functools.partial(_rglru_kernel, rev=rev, final=final, tt=tt, cb=cb, n_t=n_t),
        grid=(nc, n_t),
        in_specs=in_specs,
        out_specs=[main, pl.BlockSpec((NB, cb), lambda c, i: (0, c))],
        out_shape=[jax.ShapeDtypeStruct((n * NB, D), BF16),
                   jax.ShapeDtypeStruct((NB, D), F32)],
        scratch_shapes=[pltpu.VMEM((r + 24, cb), F32), pltpu.VMEM((r, cb), F32), pltpu.VMEM((r, cb), F32),
                        pltpu.VMEM((NB, cb), F32)],
        compiler_params=_cp("parallel", "arbitrary"),
        name="rglru_bwd" if rev else "rglru_fwd",
    )(*args)
    return out, hlast


TT = 64


def _row_perm(to_batch_major):
    r = TT * NB
    src = np.arange(r)
    t, b = src // NB, src % NB
    p = np.zeros((r, r), np.float32)
    p[b * TT + t, src] = 1.0
    return jnp.asarray(p if to_batch_major else p.T, BF16)


def _hyshort_kernel(xm, xp, xn, w, b, perm, o, *, n_t):
    i = pl.program_id(0)
    r = xm.shape[0]
    tt = r // NB
    prev = jnp.where(i == 0, 0.0, xp[...])
    nxt = jnp.where(i == n_t - 1, 0.0, xn[...])
    x = xm[...]
    sh_prev = jnp.concatenate([prev, x[:r - NB]], axis=0)
    sh_next = jnp.concatenate([x[NB:], nxt], axis=0)
    u = w[0:1, :] * sh_prev + w[1:2, :] * x + w[2:3, :] * sh_next + b[...]
    ub = _bdot(perm[...], u.astype(BF16))
    for bb in range(NB):
        o[:, bb * D:(bb + 1) * D] = ub[bb * tt:(bb + 1) * tt].astype(o.dtype)


def hyena_short(proj, n, w, b):
    tt = min(TT, n)
    assert tt == TT
    r = tt * NB
    n_t = n // tt
    c0 = OFF_HY // D
    return pl.pallas_call(
        functools.partial(_hyshort_kernel, n_t=n_t),
        grid=(n_t, 3),
        in_specs=[pl.BlockSpec((r, D), lambda i, c: (i, c0 + c)),
                  pl.BlockSpec((NB, D), lambda i, c: (jnp.maximum(i * tt - 1, 0), c0 + c)),
                  pl.BlockSpec((NB, D), lambda i, c: (jnp.minimum((i + 1) * tt, n - 1), c0 + c)),
                  pl.BlockSpec((3, D), lambda i, c: (0, c)),
                  pl.BlockSpec((1, D), lambda i, c: (0, c)),
                  pl.BlockSpec((r, r), lambda i, c: (0, 0))],
        out_specs=pl.BlockSpec((tt, NB * D), lambda i, c: (i, c)),
        out_shape=jax.ShapeDtypeStruct((n, 3 * NB * D), BF16),
        compiler_params=_cp("parallel", "parallel"),
        name="hyena_short",
    )(proj, proj, proj, w, b, _row_perm(True))


def _hyfilt_kernel(emb, w1, b1, w2, b2, fr, w3f, w3b, dcf, dcb, of, ob, h_s):
    n = emb.shape[0]

    @pl.when(pl.program_id(0) == 0)
    def _():
        f = fr[...]
        h1 = jnp.sin(f * (jnp.dot(emb[...], w1[...], precision=HI, preferred_element_type=F32) + b1[...]))
        h_s[...] = jnp.sin(f * (jnp.dot(h1, w2[...], precision=HI, preferred_element_type=F32) + b2[...]))

    h = h_s[...]
    row = lax.broadcasted_iota(jnp.int32, (n, 1), 0)
    t = row.astype(F32) * (1.0 / (n - 1))
    kf = jnp.dot(h, w3f[...], precision=HI, preferred_element_type=F32) * jnp.exp(-t * dcf[...])
    kb = jnp.dot(h, w3b[...], precision=HI, preferred_element_type=F32) * jnp.exp(-t * dcb[...])
    kb = jnp.where(row == 0, 0.0, kb)
    nrm = jnp.sum(jnp.abs(kf), axis=0, keepdims=True) + jnp.sum(jnp.abs(kb), axis=0, keepdims=True)
    inv = 1.0 / nrm
    of[...] = (kf * inv).astype(of.dtype)
    ob[...] = (kb * inv).astype(ob.dtype)


def hyena_filters(emb, p):
    n = emb.shape[0]
    tn = 256
    nj = 2 * D // tn
    full = lambda s: pl.BlockSpec(s, lambda j: (0, 0))
    return pl.pallas_call(
        _hyfilt_kernel,
        grid=(nj,),
        in_specs=[full((n, HY_PAD)), full((HY_PAD, HY_PAD)), full((1, HY_PAD)), full((HY_PAD, HY_PAD)),
                  full((1, HY_PAD)), full((1, HY_PAD)),
                  pl.BlockSpec((HY_PAD, tn), lambda j: (0, j)), pl.BlockSpec((HY_PAD, tn), lambda j: (0, nj + j)),
                  pl.BlockSpec((1, tn), lambda j: (0, j)), pl.BlockSpec((1, tn), lambda j: (0, nj + j))],
        out_specs=[pl.BlockSpec((n, tn), lambda j: (0, j)), pl.BlockSpec((n, tn), lambda j: (0, j))],
        out_shape=[jax.ShapeDtypeStruct((n, 2 * D), BF16), jax.ShapeDtypeStruct((n, 2 * D), BF16)],
        scratch_shapes=[pltpu.VMEM((n, HY_PAD), F32)],
        compiler_params=_cp("arbitrary"),
        name="hyena_filters",
    )(emb, p["w1"], p["b1"], p["w2"], p["b2"], p["freq"], p["w3"], p["w3"], p["decay"], p["decay"])


def _dftp_kernel(c_ref, s_ref, x_ref, a_ref, b_ref):
    x = x_ref[...]
    a_ref[...] = _bdot(c_ref[...], x)
    b_ref[...] = _bdot(s_ref[...], x)


def _kcombine_kernel(af_ref, ab_ref, bf_ref, bb_ref, kre_ref, kim_ref):
    i = pl.program_id(0)
    kre_ref[...] = af_ref[...] + ab_ref[...]
    bf, bb = bf_ref[...], bb_ref[...]
    first = (lax.broadcasted_iota(jnp.int32, bf.shape, 0) == 0) & (i == 0)
    kim_ref[...] = jnp.where(first, bf + bb, bb - bf)


def hyena_kspec(tabs, kf, kb):
    cmat, smat, _ = tabs
    n = cmat.shape[0]
    tf = min(1024, n)
    tn = 512
    x = jnp.concatenate([kf, kb], axis=1)
    nj = x.shape[1] // D
    out = pl.BlockSpec((tf, tn), lambda i, j: (i, j))
    a, b = pl.pallas_call(
        _dftp_kernel,
        grid=(n // tf, x.shape[1] // tn),
        in_specs=[_table_spec(tf, n), _table_spec(tf, n), pl.BlockSpec((n, tn), lambda i, j: (0, j))],
        out_specs=[out, out],
        out_shape=[jax.ShapeDtypeStruct(x.shape, F32), jax.ShapeDtypeStruct(x.shape, F32)],
        compiler_params=_cp("parallel", "arbitrary"),
        name="hyena_kdft",
    )(cmat, smat, x)
    fwd = pl.BlockSpec((tf, D), lambda i, j: (i, j))
    bwd = pl.BlockSpec((tf, D), lambda i, j: (i, nj // 2 + j))
    return pl.pallas_call(
        _kcombine_kernel,
        grid=(n // tf, nj // 2),
        in_specs=[fwd, bwd, fwd, bwd],
        out_specs=[fwd, fwd],
        out_shape=[jax.ShapeDtypeStruct((n, 2 * D), F32), jax.ShapeDtypeStruct((n, 2 * D), F32)],
        compiler_params=_cp("parallel", "parallel"),
        name="hyena_kcombine",
    )(a, a, b, b)


def _dftf_kernel(c_ref, s_ref, x_ref, kre_ref, kim_ref, qa_ref, qb_ref, *, inv_n):
    i = pl.program_id(0)
    x = x_ref[...]
    a = _bdot(c_ref[...], x)
    b = _bdot(s_ref[...], x)
    kre, kim = kre_ref[...], kim_ref[...]
    first = (lax.broadcasted_iota(jnp.int32, a.shape, 0) == 0) & (i == 0)
    qa = jnp.where(first, inv_n * (a * kre), (2.0 * inv_n) * (a * kre + b * kim))
    qb = jnp.where(first, inv_n * (b * kim), (2.0 * inv_n) * (b * kre - a * kim))
    qa_ref[...] = qa.astype(qa_ref.dtype)
    qb_ref[...] = qb.astype(qb_ref.dtype)


def _table_spec(tf, n):
    return pl.BlockSpec((tf, n), lambda i, j: (i, 0), pipeline_mode=pl.Buffered(1))


def hyena_fwd(tabs, x2d, part, kre, kim, order):
    cmat, smat, _ = tabs
    n = cmat.shape[0]
    tf = min(1024, n)
    tn = 512
    nb = D // tn
    out = pl.BlockSpec((tf, tn), lambda i, j: (i, j))
    kspec = pl.BlockSpec((tf, tn), lambda i, j: (i, order * nb + j % nb))
    return pl.pallas_call(
        functools.partial(_dftf_kernel, inv_n=1.0 / (2 * n)),
        grid=(n // tf, NB * nb),
        in_specs=[_table_spec(tf, n), _table_spec(tf, n),
                  pl.BlockSpec((n, tn), lambda i, j: (0, part * NB * nb + j)),
                  kspec, kspec],
        out_specs=[out, out],
        out_shape=[jax.ShapeDtypeStruct((n, NB * D), BF16), jax.ShapeDtypeStruct((n, NB * D), BF16)],
        compiler_params=_cp("parallel", "arbitrary"),
        name="hyena_fwd",
    )(cmat, smat, x2d, kre, kim)


def _dfti_kernel(c_ref, st_ref, qa_ref, qb_ref, xk_ref, z_ref, bias_ref, *o_refs):
    y = _bdot(c_ref[...], qa_ref[...]) + _bdot(st_ref[...], qb_ref[...])
    o = xk_ref[...].astype(F32) * (y + bias_ref[0] * z_ref[...].astype(F32))
    for o_ref in o_refs:
        o_ref[...] = o.astype(o_ref.dtype)


def hyena_inv(tabs, qa, qb, u2d, xpart, z2d, zpart, bias, order, out_dtypes):
    cmat, _, smat_t = tabs
    n = cmat.shape[0]
    tf = min(1024, n)
    tn = 512
    nb = D // tn
    out = pl.BlockSpec((tf, tn), lambda i, j: (i, j))
    q = pl.BlockSpec((n, tn), lambda i, j: (0, j))
    return pl.pallas_call(
        _dfti_kernel,
        grid=(n // tf, NB * nb),
        in_specs=[_table_spec(tf, n), _table_spec(tf, n), q, q,
                  pl.BlockSpec((tf, tn), lambda i, j: (i, xpart * NB * nb + j)),
                  pl.BlockSpec((tf, tn), lambda i, j: (i, zpart * NB * nb + j)),
                  pl.BlockSpec((1, 1, tn), lambda i, j: (order, 0, j % nb))],
        out_specs=[out] * len(out_dtypes),
        out_shape=[jax.ShapeDtypeStruct((n, NB * D), dt) for dt in out_dtypes],
        compiler_params=_cp("parallel", "arbitrary"),
        name="hyena_inv",
    )(cmat, smat_t, qa, qb, u2d, z2d, bias)


DFT_R = 64


def _dft_table_kernel(ac_ref, as_ref, bc_ref, bs_ref, c_ref, s_ref, st_ref):
    f1 = pl.program_id(0)
    ac, as_ = ac_ref[0], as_ref[0]
    bc, bs = bc_ref[...], bs_ref[...]
    c = ac * bc - as_ * bs
    s = as_ * bc + ac * bs
    row = lax.broadcasted_iota(jnp.int32, c.shape, 0)
    col = lax.broadcasted_iota(jnp.int32, c.shape, 1)
    alt_col = jnp.where(col % 2 == 0, 1.0, -1.0)
    alt_row = jnp.where(row % 2 == 0, 1.0, -1.0)
    c_ref[...] = c.astype(c_ref.dtype)
    s_ref[...] = jnp.where((row == 0) & (f1 == 0), alt_col, s).astype(s_ref.dtype)
    st_ref[...] = jnp.where(col == 0, alt_row, s).astype(st_ref.dtype)


def dft_tables(n):
    r = min(DFT_R, n)
    t = np.arange(n, dtype=np.int64)[None, :]
    f1 = np.arange(n // r, dtype=np.int64)[:, None]
    f0 = np.arange(r, dtype=np.int64)[:, None]
    ang_a = ((r * f1 * t) % (2 * n)).astype(np.float64) * (np.pi / n)
    ang_b = ((f0 * t) % (2 * n)).astype(np.float64) * (np.pi / n)
    ac = jnp.asarray(np.cos(ang_a), F32).reshape(n // r, 1, n)
    as_ = jnp.asarray(np.sin(ang_a), F32).reshape(n // r, 1, n)
    bc = jnp.asarray(np.cos(ang_b), F32)
    bs = jnp.asarray(np.sin(ang_b), F32)
    arow = pl.BlockSpec((1, 1, n), lambda i: (i, 0, 0))
    bfull = pl.BlockSpec((r, n), lambda i: (0, 0))
    out = pl.BlockSpec((r, n), lambda i: (i, 0))
    return tuple(pl.pallas_call(
        _dft_table_kernel,
        grid=(n // r,),
        in_specs=[arow, arow, bfull, bfull],
        out_specs=[out, out, out],
        out_shape=[jax.ShapeDtypeStruct((n, n), BF16)] * 3,
        compiler_params=_cp("parallel"),
        name="dft_tables",
    )(ac, as_, bc, bs))


def hyena_embedding(n):
    t = np.linspace(0.0, 1.0, n)[:, None]
    bands = np.linspace(1e-4, HY_BANDS - 1, HY_BANDS)
    ang = (2.0 * math.pi / n) * np.arange(n, dtype=np.float64)[:, None] * bands[None]
    emb = np.concatenate([t, np.cos(ang), np.sin(ang)], axis=-1)
    return jnp.asarray(np.pad(emb, ((0, 0), (0, HY_PAD - emb.shape[1]))), F32)


def hyena_mixer(proj, n, tabs, emb, p):
    u2 = hyena_short(proj, n, p["short_w"], p["short_b"])
    kf, kb = hyena_filters(emb, p)
    kre, kim = hyena_kspec(tabs, kf, kb)
    qa, qb = hyena_fwd(tabs, u2, 0, kre, kim, 0)
    (z2,) = hyena_inv(tabs, qa, qb, u2, 1, u2, 0, p["bias"], 0, (BF16,))
    qa, qb = hyena_fwd(tabs, z2, 0, kre, kim, 1)
    (y,) = hyena_inv(tabs, qa, qb, u2, 2, z2, 0, p["bias"], 1, (BF16,))
    return y


def _softplus(x):
    return jnp.maximum(x, 0.0) + jnp.log(1.0 + jnp.exp(-jnp.abs(x)))


def _split3(x):
    h = x.astype(BF16)
    r = x - h.astype(F32)
    m = r.astype(BF16)
    return h, m, (r - m.astype(F32)).astype(BF16)


def _dot_split(a, b, split_lhs):
    if split_lhs:
        q = a.shape[0]
        r = _bdot(jnp.concatenate(_split3(a), axis=0), b)
        return r[:q] + r[q:2 * q] + r[2 * q:]
    n = b.shape[1]
    r = _bdot(a, jnp.concatenate(_split3(b), axis=1))
    return r[:, :n] + r[:, n:2 * n] + r[:, 2 * n:]


def _ssd_kernel(*refs, rev, nc):
    if rev:
        (xm, dt_ref, yprev, dtb, arow, xexp, h0, y_ref, hlast, s_s) = refs
    else:
        (xm, dt_ref, dtb, arow, dsk, xexp, h0, y_ref, hlast, s_s) = refs
    q = SSM_Q
    i = pl.program_id(1)

    @pl.when(i == 0)
    def _():
        s_s[...] = h0[0]

    xbc = xm[0]
    xs = xbc[:, :D]
    dtp = _softplus(dt_ref[0] + dtb[...])
    dta = dtp * arow[...]
    row = lax.broadcasted_iota(jnp.int32, (q, q), 0)
    col = lax.broadcasted_iota(jnp.int32, (q, q), 1)
    mask = (row <= col) if rev else (row >= col)
    cum = _dot_split(mask.astype(BF16), dta, split_lhs=False)
    cum_t = cum.T
    xe = xexp[...]
    cum_e = _dot_split(cum, xe, split_lhs=True)
    dt_e = _dot_split(dtp, xe, split_lhs=True)
    last = 0 if rev else q - 1
    cum_last = cum_e[last:last + 1, :]
    xdt = xs * dt_e
    xdt_b = xdt.astype(BF16)
    s_old = s_s[...]
    s_old_b = s_old.astype(BF16)
    w_b = (jnp.exp(cum_last - cum_e) * xdt).astype(BF16)
    lane = lax.broadcasted_iota(jnp.int32, (q, 2 * SSM_P), 1)
    gw = D // SSM_G
    e_per_g = SSM_H // SSM_G
    slot0 = SSM_H if rev else 0
    y_cols = []
    s_cols = []
    for g in range(SSM_G):
        bg = xbc[:, D + g * SSM_N:D + (g + 1) * SSM_N].astype(BF16)
        cg = xbc[:, D + SSM_G * SSM_N + g * SSM_N:D + SSM_G * SSM_N + (g + 1) * SSM_N].astype(BF16)
        cbm = lax.dot_general(cg, bg, (((1,), (1,)), ((), ())), preferred_element_type=F32)
        for pr in range(e_per_g // 2):
            c0 = g * gw + pr * 2 * SSM_P
            xpair = xdt[:, c0:c0 + 2 * SSM_P]
            mhs, xhs = [], []
            for hh in range(2):
                slot = slot0 + g * e_per_g + pr * 2 + hh
                seg = cum[:, slot:slot + 1] - cum_t[slot:slot + 1, :]
                dec = jnp.exp(jnp.where(mask, seg, -jnp.inf))
                mhs.append((cbm * dec).astype(BF16))
                in_head = (lane >= hh * SSM_P) & (lane < (hh + 1) * SSM_P)
                xhs.append(jnp.where(in_head, xpair, 0.0).astype(BF16))
            y_cols.append(_bdot(jnp.concatenate(mhs, axis=1), jnp.concatenate(xhs, axis=0)))
        gs = slice(g * gw, (g + 1) * gw)
        y_state = _bdot(cg, s_old_b[:, gs])
        y_cols[-(e_per_g // 2):] = [
            y_cols[-(e_per_g // 2) + pr] + jnp.exp(cum_e[:, g * gw + pr * 2 * SSM_P:g * gw + (pr + 1) * 2 * SSM_P])
            * y_state[:, pr * 2 * SSM_P:(pr + 1) * 2 * SSM_P] for pr in range(e_per_g // 2)]
        s_new = jnp.exp(cum_last[:, gs]) * s_old[:, gs] + lax.dot_general(
            bg, w_b[:, gs], (((0,), (0,)), ((), ())), preferred_element_type=F32)
        s_cols.append(s_new)
    y = jnp.concatenate(y_cols, axis=1)
    s_new = jnp.concatenate(s_cols, axis=1)
    s_s[...] = s_new
    hlast[0] = s_new
    if rev:
        y_ref[0] = (y + yprev[0]).astype(y_ref.dtype)
    else:
        y_ref[0] = y + dsk[...] * xs


def _scan_perm(n, col_major, inverse):
    r_ = min(n // GRID_W, TT) if col_major else TT
    s_ = TT // r_
    src = np.arange(TT * NB)
    b, rs = src % NB, src // NB
    r, s = rs // s_, rs % s_
    p = np.zeros((TT * NB, TT * NB), np.float32)
    p[b * TT + s * r_ + r, src] = 1.0
    return jnp.asarray(p.T if inverse else p, BF16)


def _scan_in_kernel(x_ref, xp_ref, xn_ref, cw_ref, cb_ref, perm_ref, o_ref, *, n_k):
    k = pl.program_id(0)
    r = TT * NB
    c = x_ref.shape[-1]
    x = x_ref[...].reshape(r, c)
    prev = jnp.where(k == 0, 0.0, xp_ref[...].reshape(NB * NB, c)[NB * NB - 2 * NB:, :D_XBC])
    nxt = jnp.where(k == n_k - 1, 0.0, xn_ref[...].reshape(NB * NB, c)[:NB, :D_XBC])
    ext = jnp.concatenate([prev, x[:, :D_XBC], nxt], axis=0)
    acc = cb_ref[...]
    for j in range(4):
        acc = acc + cw_ref[j:j + 1, :] * ext[NB * j:NB * j + r]
    x = jnp.concatenate([acc * jax.nn.sigmoid(acc), x[:, D_XBC:]], axis=1)
    hi = x.astype(BF16)
    lo = (x - hi.astype(F32)).astype(BF16)
    p = perm_ref[...]
    y = _bdot(p, hi) + _bdot(p, lo)
    for bb in range(NB):
        o_ref[bb] = y[bb * TT:(bb + 1) * TT]


def _tile_spec(n, col_major, width, cblk):
    if col_major:
        rows = n // GRID_W
        r_ = min(rows, TT)
        assert r_ == rows, "a scan tile must cover whole grid columns"
        return (rows, GRID_W, NB), pl.BlockSpec((rows, TT // r_, NB, width), lambda k: (0, k, 0, cblk))
    return (n // TT, TT, NB), pl.BlockSpec((1, TT, NB, width), lambda k: (k, 0, 0, cblk))


def scan_in(proj, n, col_major, conv_w, conv_b):
    lead, spec = _tile_spec(n, col_major, D_SSD, OFF_XBC // D_SSD)
    cblk = OFF_XBC // D_SSD
    n_k = n // TT
    last8 = TT // NB - 1
    if col_major:
        assert n // GRID_W == TT
        prev = pl.BlockSpec((NB, 1, NB, D_SSD), lambda k: (last8, jnp.maximum(k - 1, 0), 0, cblk))
        nxt = pl.BlockSpec((NB, 1, NB, D_SSD), lambda k: (0, jnp.minimum(k + 1, n_k - 1), 0, cblk))
    else:
        prev = pl.BlockSpec((1, NB, NB, D_SSD), lambda k: (jnp.maximum(k - 1, 0), last8, 0, cblk))
        nxt = pl.BlockSpec((1, NB, NB, D_SSD), lambda k: (jnp.minimum(k + 1, n_k - 1), 0, 0, cblk))
    p4 = proj.reshape(lead + (proj.shape[-1],))
    return pl.pallas_call(
        functools.partial(_scan_in_kernel, n_k=n_k),
        grid=(n_k,),
        in_specs=[spec, prev, nxt, pl.BlockSpec((4, D_XBC), lambda k: (0, 0)), pl.BlockSpec((1, D_XBC), lambda k: (0, 0)),
                  pl.BlockSpec((TT * NB, TT * NB), lambda k: (0, 0))],
        out_specs=pl.BlockSpec((NB, TT, D_SSD), lambda k: (0, k, 0)),
        out_shape=jax.ShapeDtypeStruct((NB, n, D_SSD), F32),
        compiler_params=_cp("parallel"),
        name="scan_in",
    )(p4, p4, p4, conv_w, conv_b, _scan_perm(n, col_major, False))


def _scan_out_kernel(y_ref, perm_ref, o_ref):
    y = jnp.concatenate([y_ref[bb] for bb in range(NB)], axis=0)
    o_ref[...] = _bdot(perm_ref[...], y).reshape(o_ref.shape)


def scan_out(y, n, col_major):
    lead, spec = _tile_spec(n, col_major, D, 0)
    out = pl.pallas_call(
        _scan_out_kernel,
        grid=(n // TT,),
        in_specs=[pl.BlockSpec((NB, TT, D), lambda k: (0, k, 0)), pl.BlockSpec((TT * NB, TT * NB), lambda k: (0, 0))],
        out_specs=spec,
        out_shape=jax.ShapeDtypeStruct(lead + (D,), F32),
        compiler_params=_cp("parallel"),
        name="scan_out",
    )(y, _scan_perm(n, col_major, True))
    return out.reshape(n * NB, D)


def ssd_pass(xs, rev, h0, p, yprev=None):
    _, n, _ = xs.shape
    q = SSM_Q
    nc = n // q
    d_idx = 1 if rev else 0

    def cix(i):
        return (nc - 1 - i) if rev else i

    in_specs = [pl.BlockSpec((1, q, D_XBC), lambda b, i: (b, cix(i), 0)),
                pl.BlockSpec((1, q, 128), lambda b, i: (b, cix(i), D_XBC // 128))]
    args = [xs, xs]
    ymain = pl.BlockSpec((1, q, D), lambda b, i: (b, cix(i), 0))
    if rev:
        in_specs.append(ymain)
        args.append(yprev)
    full = lambda s: pl.BlockSpec(s, lambda b, i: (0,) * len(s))
    in_specs += [full((1, 128)), full((1, 128))]
    args += [p["dt_bias"], p["arow"][d_idx]]
    if not rev:
        in_specs.append(full((1, D)))
        args.append(p["d_skip"])
    in_specs += [full((128, D)), pl.BlockSpec((1, SSM_N, D), lambda b, i: (b, 0, 0))]
    args += [p["xexp"][d_idx], h0]
    y, hlast = pl.pallas_call(
        functools.partial(_ssd_kernel, rev=rev, nc=nc),
        grid=(NB, nc),
        in_specs=in_specs,
        out_specs=[ymain, pl.BlockSpec((1, SSM_N, D), lambda b, i: (b, 0, 0))],
        out_shape=[jax.ShapeDtypeStruct((NB, n, D), BF16 if rev else F32), jax.ShapeDtypeStruct((NB, SSM_N, D), F32)],
        scratch_shapes=[pltpu.VMEM((SSM_N, D), F32)],
        compiler_params=_cp("parallel", "arbitrary"),
        name="ssd_bwd" if rev else "ssd_fwd",
    )(*args)
    return y, hlast


def ssd_bidir(xs, hf0, hb0, p):
    yf, sf = ssd_pass(xs, False, hf0, p)
    y, sb = ssd_pass(xs, True, hb0, p, yprev=yf)
    return y, sf, sb


def _merge_kernel(x_ref, gr_ref, gh_ref, gs_ref, yr_ref, yh_ref, ys_ref, z_ref, nw_ref, wb_ref, wo_ref, g1_ref,
                  perm_ref, o_ref):
    z = z_ref[...]
    ys = _rms(ys_ref[...].astype(F32) * (z * jax.nn.sigmoid(z)), nw_ref[...])
    m = jax.nn.sigmoid(gr_ref[...]) * _bdot(yr_ref[...], wb_ref[0])
    yh = jnp.concatenate([yh_ref[:, bb * D:(bb + 1) * D] for bb in range(NB)], axis=0)
    yh = _bdot(perm_ref[...], yh).astype(BF16)
    m = m + jax.nn.sigmoid(gh_ref[...]) * _bdot(yh, wb_ref[1])
    m = m + jax.nn.sigmoid(gs_ref[...]) * _bdot(ys.astype(BF16), wb_ref[2])
    o = _bdot(m.astype(BF16), wo_ref[...])
    o_ref[...] = x_ref[...] + _mod_rows(o, g1_ref[...], None)


def merge(x, proj, yr, yh, ys, norm_w, wb, wo, mod, mrow):
    m = x.shape[0]
    row = lambda w, c: pl.BlockSpec((TM, w), lambda i: (i, c))
    return pl.pallas_call(
        _merge_kernel,
        grid=(m // TM,),
        in_specs=[row(D, 0), row(D, OFF_GT // D), row(D, OFF_GT // D + 1), row(D, OFF_GT // D + 2), row(D, 0),
                  pl.BlockSpec((TM // NB, NB * D), lambda i: (i, 0)), row(D, 0), row(D, OFF_SZ // D),
                  pl.BlockSpec((1, D), lambda i: (0, 0)),
                  pl.BlockSpec((3, D, D), lambda i: (0, 0, 0)),
                  pl.BlockSpec((D, D), lambda i: (0, 0)),
                  pl.BlockSpec((NB, D), lambda i: (mrow, 2)),
                  pl.BlockSpec((TM, TM), lambda i: (0, 0))],
        out_specs=row(D, 0),
        out_shape=jax.ShapeDtypeStruct((m, D), F32),
        compiler_params=_cp("parallel"),
        name="merge",
    )(x, proj, proj, proj, yr, yh, ys, proj, norm_w, wb, wo, mod, _row_perm(False))


def _ffn_kernel(x_ref, g_ref, sh_ref, sc_ref, g2_ref, wg_ref, wu_ref, wd_ref, fn_ref, o_ref, xn_s, acc_s, *, nf, final):
    f = pl.program_id(1)

    @pl.when(f == 0)
    def _():
        y = _rms(x_ref[...], g_ref[...])
        xn_s[...] = _mod_rows(y, 1.0 + sc_ref[...], sh_ref[...]).astype(BF16)
        acc_s[...] = jnp.zeros_like(acc_s)

    xn = xn_s[...]
    hg = _bdot(xn, wg_ref[...])
    hu = _bdot(xn, wu_ref[...])
    hh = (hg * jax.nn.sigmoid(hg) * hu).astype(BF16)
    acc_s[...] += _bdot(hh, wd_ref[...])

    @pl.when(f == nf - 1)
    def _():
        o = x_ref[...] + _mod_rows(acc_s[...], g2_ref[...], None)
        if final:
            o = _rms(o, fn_ref[...])
        o_ref[...] = o


def ffn(x, g, mod, mrow, w_up, w_down, final_norm, final):
    m = x.shape[0]
    d_ff = w_down.shape[0]
    nf = 2
    tf = d_ff // nf
    tm = min(1024, m)
    vec = pl.BlockSpec((1, D), lambda i, f: (0, 0))
    return pl.pallas_call(
        functools.partial(_ffn_kernel, nf=nf, final=final),
        grid=(m // tm, nf),
        in_specs=[pl.BlockSpec((tm, D), lambda i, f: (i, 0)), vec,
                  pl.BlockSpec((NB, D), lambda i, f: (mrow, 3)),
                  pl.BlockSpec((NB, D), lambda i, f: (mrow, 4)),
                  pl.BlockSpec((NB, D), lambda i, f: (mrow, 5)),
                  pl.BlockSpec((D, tf), lambda i, f: (0, f)),
                  pl.BlockSpec((D, tf), lambda i, f: (0, nf + f)),
                  pl.BlockSpec((tf, D), lambda i, f: (f, 0)), vec],
        out_specs=pl.BlockSpec((tm, D), lambda i, f: (i, 0)),
        out_shape=jax.ShapeDtypeStruct((m, D), F32),
        scratch_shapes=[pltpu.VMEM((tm, D), BF16), pltpu.VMEM((tm, D), F32)],
        compiler_params=_cp("parallel", "arbitrary"),
        name="ffn",
    )(x, g, mod, mod, mod, w_up, w_up, w_down, final_norm)


def _pack_w_in(w):
    o_sp = 6 * D
    o_gt = o_sp + D_XBC + 2 * SSM_H
    dt = jnp.pad(w[:, o_sp + D_XBC:o_gt], ((0, 0), (0, OFF_GT - OFF_DT - 2 * SSM_H)))
    out = jnp.concatenate([w[:, :o_sp], w[:, o_sp:o_sp + D_XBC], dt, w[:, o_gt:o_gt + 3 * D]], axis=1)
    assert out.shape[1] == D_INP
    return out.astype(BF16)


def _pad_to(a, shape):
    return jnp.pad(a, [(0, s - d) for s, d in zip(shape, a.shape)])


def kernel(x, c, ctx, c_ctx, w_mod, b_mod, norm_mix, norm_ffn, w_in, rnn_conv_w, rnn_conv_b, rnn_gate_a_w, rnn_gate_a_b, rnn_gate_x_w, rnn_gate_x_b, rnn_lambda, hy_short_w, hy_short_b, hy_w1, hy_b1, hy_w2, hy_b2, hy_w3, hy_freq, hy_decay, hy_bias, ssm_conv_w, ssm_conv_b, ssm_a_log, ssm_dt_bias, ssm_d, ssm_norm, w_branch, w_out, w_up, w_down, final_norm):
    bsz, n_l, d = x.shape
    n_c = ctx.shape[1]
    depth = w_in.shape[0]
    assert bsz == NB and d == D

    xl = x.transpose(1, 0, 2).reshape(n_l * NB, D)
    sc = ctx.transpose(1, 0, 2).reshape(n_c * NB, D)
    c16 = jnp.concatenate([c, jnp.broadcast_to(c_ctx[None], (NB, D))], axis=0)
    mods = modulation(c16, w_mod, b_mod)

    tabs = {n_l: dft_tables(n_l), n_c: dft_tables(n_c)}
    embs = {n_l: hyena_embedding(n_l), n_c: hyena_embedding(n_c)}
    head_lane = jnp.arange(128)[:, None]
    hp_head = (jnp.arange(D) // SSM_P)[None, :]
    xexp = jnp.stack([(head_lane == hp_head + SSM_H * dd).astype(BF16) for dd in range(2)])
    zeros_h = jnp.zeros((NB, D), F32)
    zeros_s = jnp.zeros((NB, SSM_N, D), F32)
    fn = final_norm.reshape(1, D)

    for i in range(depth):
        ctx_out = i < depth - 1
        mod = mods[i]
        w_i = _pack_w_in(w_in[i])
        rp = dict(conv_w=rnn_conv_w[i], conv_b=rnn_conv_b[i].reshape(1, D),
                  wa=rnn_gate_a_w[i].astype(BF16), wx=rnn_gate_x_w[i].astype(BF16),
                  ba=rnn_gate_a_b[i].reshape(2, 1, D), bx=rnn_gate_x_b[i].reshape(2, 1, D),
                  lam=rnn_lambda[i].reshape(2, 1, D))
        hp = dict(short_w=hy_short_w[i], short_b=hy_short_b[i].reshape(1, 3 * D),
                  w1=_pad_to(hy_w1[i], (HY_PAD, HY_PAD)), b1=_pad_to(hy_b1[i].reshape(1, -1), (1, HY_PAD)),
                  w2=_pad_to(hy_w2[i], (HY_PAD, HY_PAD)), b2=_pad_to(hy_b2[i].reshape(1, -1), (1, HY_PAD)),
                  freq=_pad_to(hy_freq[i].reshape(1, -1), (1, HY_PAD)),
                  w3=_pad_to(hy_w3[i], (HY_PAD, 4 * D)), decay=hy_decay[i].reshape(1, 4 * D),
                  bias=hy_bias[i].reshape(2, 1, D))
        a_neg = -jnp.exp(ssm_a_log[i].astype(F32))
        arow = jnp.stack([_pad_to(jnp.concatenate([jnp.zeros((SSM_H * dd,), F32), a_neg[dd]])[None], (1, 128))
                          for dd in range(2)])
        sp = dict(conv_w=ssm_conv_w[i], conv_b=ssm_conv_b[i].reshape(1, D_XBC),
                  dt_bias=_pad_to(ssm_dt_bias[i].reshape(1, 2 * SSM_H), (1, 128)), arow=arow,
                  d_skip=jnp.repeat(ssm_d[i], SSM_P).reshape(1, D), xexp=xexp)
        wb = w_branch[i].astype(BF16)
        wo = w_out[i].astype(BF16)
        wu = w_up[i].astype(BF16)
        wd = w_down[i].astype(BF16)
        g_mix = norm_mix[i].reshape(1, D)
        g_ffn = norm_ffn[i].reshape(1, D)
        nrm_s = ssm_norm[i].reshape(1, D)

        proj_c = in_proj(sc, g_mix, mod, 1, w_i)
        proj_l = in_proj(xl, g_mix, mod, 0, w_i)

        hf_c, hf_last = rglru_pass(proj_c, n_c, 0, False, zeros_h, rp)
        yr_c, hb_last = rglru_pass(proj_c, n_c, 1, True, zeros_h, rp, hf=hf_c)
        hf_l, _ = rglru_pass(proj_l, n_l, 0, False, hf_last, rp)
        yr_l, _ = rglru_pass(proj_l, n_l, 1, True, hb_last, rp, hf=hf_l)

        ys_c, sf, sb = ssd_bidir(scan_in(proj_c, n_c, False, sp["conv_w"], sp["conv_b"]), zeros_s, zeros_s, sp)
        ys_l, _, _ = ssd_bidir(scan_in(proj_l, n_l, True, sp["conv_w"], sp["conv_b"]), sf, sb, sp)
        ys_l = scan_out(ys_l, n_l, True)

        yh_l = hyena_mixer(proj_l, n_l, tabs[n_l], embs[n_l], hp)
        xl = merge(xl, proj_l, yr_l, yh_l, ys_l, nrm_s, wb, wo, mod, 0)
        xl = ffn(xl, g_ffn, mod, 0, wu, wd, fn, final=not ctx_out)

        if ctx_out:
            ys_c = scan_out(ys_c, n_c, False)
            yh_c = hyena_mixer(proj_c, n_c, tabs[n_c], embs[n_c], hp)
            sc = merge(sc, proj_c, yr_c, yh_c, ys_c, nrm_s, wb, wo, mod, 1)
            sc = ffn(sc, g_ffn, mod, 1, wu, wd, fn, final=False)

    return xl.reshape(n_l, NB, D).transpose(1, 0, 2)
```
